```python
import math
import jax, jax.numpy as jnp
from jax import lax
import numpy as np

D_MODEL = 1024
BATCH = 2
SEQ = 8192
DEPTH = 2

CHUNK = 64
N_META = 16
N_A_LAYERS = DEPTH // 2
N_B_LAYERS = DEPTH - N_A_LAYERS
GDN_HEADS = 8
GDN_HEAD_DIM = 128
GDN_WIDTH = GDN_HEADS * GDN_HEAD_DIM
CONV_WIDTH = 4
SB_HEADS = 16
SB_HEAD_DIM = 64
SB_WIDTH = SB_HEADS * SB_HEAD_DIM
SB_BLOCK = 128
D_FF = -(-(8 * D_MODEL) // (3 * 256)) * 256
GDN_IN_COLS = 4 * GDN_WIDTH + 2 * GDN_HEADS
EPS = 1e-6

kernel_name = "yoco_gdn_stickbreaking_hybrid"


def rms_norm(x, g):
    xf = x.astype(jnp.float32)
    y = xf * lax.rsqrt(jnp.mean(xf * xf, axis=-1, keepdims=True) + EPS)
    return (y * g.astype(jnp.float32)).astype(x.dtype)


def l2_normalize(x):
    return x * lax.rsqrt(jnp.sum(x * x, axis=-1, keepdims=True) + EPS)


def causal_depthwise_conv(x, w):
    L = x.shape[1]
    xp = jnp.pad(x, ((0, 0), (CONV_WIDTH - 1, 0), (0, 0)))
    return sum(xp[:, i:i + L] * w[i] for i in range(CONV_WIDTH))


def gated_delta_chunked(q, k, v, g, beta):
    b, nh, L, dk = q.shape
    dv = v.shape[-1]
    n = L // CHUNK
    q = q.reshape(b, nh, n, CHUNK, dk) * (dk ** -0.5)
    k = k.reshape(b, nh, n, CHUNK, dk)
    v = v.reshape(b, nh, n, CHUNK, dv)
    g = g.reshape(b, nh, n, CHUNK)
    beta = beta.reshape(b, nh, n, CHUNK)
    G = jnp.cumsum(g, axis=-1)
    incl = jnp.tril(jnp.ones((CHUNK, CHUNK), dtype=bool))
    strict = jnp.tril(jnp.ones((CHUNK, CHUNK), dtype=bool), -1)
    diff = G[..., :, None] - G[..., None, :]
    decay = jnp.where(incl, jnp.exp(jnp.where(incl, diff, 0.0)), 0.0)
    kk = jnp.einsum('bhnid,bhnjd->bhnij', k, k)
    lower = jnp.eye(CHUNK, dtype=q.dtype) + jnp.where(strict, beta[..., :, None] * decay * kk, 0.0)
    rhs = jnp.concatenate([beta[..., None] * v, (beta * jnp.exp(G))[..., None] * k], axis=-1)
    sol = lax.linalg.triangular_solve(lower, rhs, left_side=True, lower=True, unit_diagonal=True)
    u_base = sol[..., :dv]
    w_corr = sol[..., dv:]
    attn = jnp.einsum('bhnid,bhnjd->bhnij', q, k) * decay
    q_dec = q * jnp.exp(G)[..., None]
    g_last = G[..., -1]
    k_dec = k * jnp.exp(g_last[..., None] - G)[..., None]

    def step(S, inp):
        u_b, w_c, a_c, qd, kd, gl = inp
        u = u_b - jnp.einsum('bhcd,bhde->bhce', w_c, S)
        o = jnp.einsum('bhcd,bhde->bhce', qd, S) + jnp.einsum('bhij,bhje->bhie', a_c, u)
        S = S * jnp.exp(gl)[..., None, None] + jnp.einsum('bhcd,bhce->bhde', kd, u)
        return S, o

    xs = tuple(jnp.moveaxis(t, 2, 0) for t in (u_base, w_corr, attn, q_dec, k_dec, g_last))
    S0 = jnp.zeros((b, nh, dk, dv), dtype=q.dtype)
    _, o = lax.scan(step, S0, xs)
    return jnp.moveaxis(o, 0, 2).reshape(b, nh, L, dv)


def gdn_mixer(h, w_in, conv_w, a_log, dt_bias, onorm_g, w_out):
    b, L, _ = h.shape
    proj = h @ w_in
    qkv = jax.nn.silu(causal_depthwise_conv(proj[..., :3 * GDN_WIDTH], conv_w)).astype(jnp.float32)
    gate = proj[..., 3 * GDN_WIDTH:4 * GDN_WIDTH].astype(jnp.float32)
    a_in = proj[..., 4 * GDN_WIDTH:4 * GDN_WIDTH + GDN_HEADS].astype(jnp.float32)
    b_in = proj[..., 4 * GDN_WIDTH + GDN_HEADS:].astype(jnp.float32)
    heads = lambda t: t.reshape(b, L, GDN_HEADS, GDN_HEAD_DIM).transpose(0, 2, 1, 3)
    q = l2_normalize(heads(qkv[..., :GDN_WIDTH]))
    k = l2_normalize(heads(qkv[..., GDN_WIDTH:2 * GDN_WIDTH]))
    v = heads(qkv[..., 2 * GDN_WIDTH:])
    g = (-jnp.exp(a_log.astype(jnp.float32)) * jax.nn.softplus(a_in + dt_bias.astype(jnp.float32))).transpose(0, 2, 1)
    beta = jax.nn.sigmoid(b_in).transpose(0, 2, 1)
    front = (-N_META) % CHUNK
    back = (-(front + L)) % CHUNK
    p4 = ((0, 0), (0, 0), (front, back), (0, 0))
    p3 = ((0, 0), (0, 0), (front, back))
    o = gated_delta_chunked(jnp.pad(q, p4), jnp.pad(k, p4), jnp.pad(v, p4), jnp.pad(g, p3), jnp.pad(beta, p3))
    o = o[:, :, front:front + L].transpose(0, 2, 1, 3)
    o = o * lax.rsqrt(jnp.mean(o * o, axis=-1, keepdims=True) + EPS) * onorm_g.astype(jnp.float32)
    o = o * jax.nn.silu(gate.reshape(b, L, GDN_HEADS, GDN_HEAD_DIM))
    return o.reshape(b, L, GDN_WIDTH).astype(h.dtype) @ w_out


def shared_kv(h, norm_g, w_kv):
    b, L, _ = h.shape
    kv = (rms_norm(h, norm_g) @ w_kv).astype(jnp.float32).reshape(b, L, 2, SB_HEADS, SB_HEAD_DIM)
    return kv[:, :, 0].transpose(0, 2, 1, 3), kv[:, :, 1].transpose(0, 2, 1, 3)


def sb_mixer(h, k_sh, v_sh, w_q, w_o):
    b, L, _ = h.shape
    q = (h @ w_q).astype(jnp.float32).reshape(b, L, SB_HEADS, SB_HEAD_DIM).transpose(0, 2, 1, 3)
    pad = (-L) % SB_BLOCK
    Lp = L + pad
    p4 = ((0, 0), (0, 0), (0, pad), (0, 0))
    q, k, v = jnp.pad(q, p4), jnp.pad(k_sh, p4), jnp.pad(v_sh, p4)
    nb = Lp // SB_BLOCK
    q_blocks = q.reshape(b, SB_HEADS, nb, SB_BLOCK, SB_HEAD_DIM).transpose(2, 0, 1, 3, 4)
    key_pos = jnp.arange(Lp)
    scale = SB_HEAD_DIM ** -0.5

    def one_block(args):
        q_blk, blk = args
        q_pos = blk * SB_BLOCK + jnp.arange(SB_BLOCK)
        visible = key_pos[None, :] < q_pos[:, None]
        z = jnp.einsum('bhqd,bhkd->bhqk', q_blk, k) * scale
        log_beta = jax.nn.log_sigmoid(z)
        log_keep = jnp.where(visible, log_beta - z, 0.0)
        log_w = log_beta + lax.cumsum(log_keep, axis=3, reverse=True) - log_keep
        w = jnp.where(visible, jnp.exp(log_w), 0.0)
        return jnp.einsum('bhqk,bhkd->bhqd', w, v)

    o = lax.map(one_block, (q_blocks, jnp.arange(nb)))
    o = o.transpose(1, 2, 0, 3, 4).reshape(b, SB_HEADS, Lp, SB_HEAD_DIM)[:, :, :L]
    o = o.transpose(0, 2, 1, 3).reshape(b, L, SB_WIDTH).astype(h.dtype)
    return o @ w_o


def swiglu(h, w_gate_up, w_down):
    gu = h @ w_gate_up
    return (jax.nn.silu(gu[..., :D_FF]) * gu[..., D_FF:]) @ w_down


def setup_inputs(seed: int = 0) -> dict:
    key = jax.random.key(seed)
    ks = jax.random.split(key, 20)
    f32 = jnp.float32
    dense = lambda k, shape, fan_in: jax.random.normal(k, shape, f32) * (fan_in ** -0.5)
    gain = lambda k, shape: 1.0 + 0.02 * jax.random.normal(k, shape, f32)
    dt = jnp.exp(jax.random.uniform(ks[5], (N_A_LAYERS, GDN_HEADS), f32, math.log(1e-3), math.log(1e-1)))
    return {
        "x": jax.random.normal(ks[0], (BATCH, SEQ, D_MODEL), f32),
        "meta_tokens": jax.random.normal(ks[1], (N_META, D_MODEL), f32),
        "gdn_norm_g": gain(ks[2], (N_A_LAYERS, D_MODEL)),
        "gdn_w_in": dense(ks[3], (N_A_LAYERS, D_MODEL, GDN_IN_COLS), D_MODEL),
        "gdn_conv_w": dense(ks[4], (N_A_LAYERS, CONV_WIDTH, 3 * GDN_WIDTH), CONV_WIDTH),
        "gdn_a_log": jnp.log(jax.random.uniform(ks[6], (N_A_LAYERS, GDN_HEADS), f32, 1.0, 16.0)),
        "gdn_dt_bias": dt + jnp.log(-jnp.expm1(-dt)),
        "gdn_onorm_g": gain(ks[7], (N_A_LAYERS, GDN_HEAD_DIM)),
        "gdn_w_out": dense(ks[8], (N_A_LAYERS, GDN_WIDTH, D_MODEL), GDN_WIDTH),
        "kv_norm_g": gain(ks[9], (D_MODEL,)),
        "w_kv": dense(ks[10], (D_MODEL, 2 * SB_WIDTH), D_MODEL),
        "sb_norm_g": gain(ks[11], (N_B_LAYERS, D_MODEL)),
        "sb_w_q": dense(ks[12], (N_B_LAYERS, D_MODEL, SB_WIDTH), D_MODEL),
        "sb_w_o": dense(ks[13], (N_B_LAYERS, SB_WIDTH, D_MODEL), SB_WIDTH),
        "ffn_norm_g": gain(ks[14], (DEPTH, D_MODEL)),
        "ffn_w_gate_up": dense(ks[15], (DEPTH, D_MODEL, 2 * D_FF), D_MODEL),
        "ffn_w_down": dense(ks[16], (DEPTH, D_FF, D_MODEL), D_FF),
        "final_norm_g": gain(ks[17], (D_MODEL,)),
    }


def reference(x, meta_tokens, gdn_norm_g, gdn_w_in, gdn_conv_w, gdn_a_log, gdn_dt_bias,
              gdn_onorm_g, gdn_w_out, kv_norm_g, w_kv, sb_norm_g, sb_w_q, sb_w_o,
              ffn_norm_g, ffn_w_gate_up, ffn_w_down, final_norm_g):
    b = x.shape[0]
    meta = jnp.broadcast_to(meta_tokens.astype(x.dtype)[None], (b, N_META, D_MODEL))
    h = jnp.concatenate([meta, x], axis=1)
    k_sh = None
    v_sh = None
    for layer in range(DEPTH):
        if layer < N_A_LAYERS:
            h = h + gdn_mixer(rms_norm(h, gdn_norm_g[layer]), gdn_w_in[layer], gdn_conv_w[layer],
                              gdn_a_log[layer], gdn_dt_bias[layer], gdn_onorm_g[layer], gdn_w_out[layer])
        else:
            j = layer - N_A_LAYERS
            h = h + sb_mixer(rms_norm(h, sb_norm_g[j]), k_sh, v_sh, sb_w_q[j], sb_w_o[j])
        h = h + swiglu(rms_norm(h, ffn_norm_g[layer]), ffn_w_gate_up[layer], ffn_w_down[layer])
        if layer == N_A_LAYERS - 1:
            k_sh, v_sh = shared_kv(h, kv_norm_g, w_kv)
    return rms_norm(h, final_norm_g)[:, N_META:]
```

```python
import functools

import jax
import jax.numpy as jnp
from jax import lax
from jax.experimental import pallas as pl
from jax.experimental.pallas import tpu as pltpu

F32 = jnp.float32
BF16 = jnp.bfloat16

D_MODEL = 1024
N_META = 16
GDN_HEADS = 8
GDN_HEAD_DIM = 128
GDN_WIDTH = GDN_HEADS * GDN_HEAD_DIM
CONV_WIDTH = 4
SB_HEADS = 16
SB_HEAD_DIM = 64
SB_WIDTH = SB_HEADS * SB_HEAD_DIM
D_FF = 2816
EPS = 1e-6

LANES = 128
SUBLANES = 8
GCHUNK = 128
FRONT = GCHUNK - N_META
SB_BLOCK = 128
SB_SKIP = 100.0
VMEM_LIMIT = 56 * 1024 * 1024


def _dot(a, b):
    return lax.dot_general(a, b, (((1,), (0,)), ((), ())), preferred_element_type=F32)


def _dot_nt(a, b):
    return lax.dot_general(a, b, (((1,), (1,)), ((), ())), preferred_element_type=F32)


def _split(a):
    hi = a.astype(BF16)
    lo = (a - hi.astype(F32)).astype(BF16)
    return hi, lo


def _dot3(a, b):
    ah, al = _split(a)
    bh, bl = _split(b)
    return _dot(ah, bh) + (_dot(ah, bl) + _dot(al, bh))


def _dot_exact_lhs(m_bf16, b):
    bh, bl = _split(b)
    return _dot(m_bf16, bh) + _dot(m_bf16, bl)


def _dot_exact_rhs(a, m_bf16):
    ah, al = _split(a)
    return _dot(ah, m_bf16) + _dot(al, m_bf16)


def _rms(x, g):
    return x * lax.rsqrt(jnp.mean(x * x, axis=-1, keepdims=True) + EPS) * g


def _sigmoid(x):
    return 1.0 / (1.0 + jnp.exp(-x))


def _softplus(x):
    return jnp.maximum(x, 0.0) + jnp.log(1.0 + jnp.exp(-jnp.abs(x)))


def _params(sem):
    return pltpu.CompilerParams(dimension_semantics=sem, vmem_limit_bytes=VMEM_LIMIT)


def _gdn_in_kernel(h_ref, ng_ref, w_ref, wab_ref, cw_ref, ab_ref,
                   q_ref, k_ref, v_ref, gate_ref, gb_ref,
                   carry_ref, cbuf_ref, *, tm, tiles_per_batch):
    tb = pl.program_id(0) % tiles_per_batch

    @pl.when(tb == 0)
    def _():
        carry_ref[...] = jnp.zeros_like(carry_ref)

    xn = _rms(h_ref[...], ng_ref[...]).astype(BF16)
    outs = (q_ref, k_ref, v_ref)
    for part in range(3):
        cols = slice(part * GDN_WIDTH, (part + 1) * GDN_WIDTH)
        p = _dot(xn, w_ref[:, cols])
        cbuf_ref[0:SUBLANES, :] = carry_ref[part]
        cbuf_ref[SUBLANES:, :] = p
        carry_ref[part] = p[tm - SUBLANES:, :]
        cw = cw_ref[:, cols]
        y = p * cw[CONV_WIDTH - 1:CONV_WIDTH, :]
        for i in range(1, CONV_WIDTH):
            y = y + cbuf_ref[pl.ds(SUBLANES - i, tm), :] * cw[CONV_WIDTH - 1 - i:CONV_WIDTH - i, :]
        y = y * _sigmoid(y)
        if part < 2:
            for hd in range(GDN_HEADS):
                sl = slice(hd * GDN_HEAD_DIM, (hd + 1) * GDN_HEAD_DIM)
                seg = y[:, sl]
                outs[part][:, sl] = seg * lax.rsqrt(jnp.sum(seg * seg, axis=-1, keepdims=True) + EPS)
        else:
            v_ref[...] = y
    gate_ref[...] = _dot(xn, w_ref[:, 3 * GDN_WIDTH:4 * GDN_WIDTH])

    ab = _dot(xn, wab_ref[...])
    a_log = ab_ref[0:1, :]
    dt_bias = ab_ref[1:2, :]
    g = -jnp.exp(a_log) * _softplus(ab + dt_bias)
    beta = _sigmoid(ab)
    lane = lax.broadcasted_iota(jnp.int32, ab.shape, 1)
    pos = tb * tm + lax.broadcasted_iota(jnp.int32, ab.shape, 0)
    gb_ref[...] = jnp.where(pos >= FRONT, jnp.where(lane < GDN_HEADS, g, beta), 0.0)


def _gdn_in(h, ng, w_main, w_ab, conv_w, ab_par, *, tm, lp):
    m = h.shape[0]
    row = lambda r: (r, 0)
    const = lambda r: (0, 0)
    wide = pl.BlockSpec((tm, GDN_WIDTH), row)
    return pl.pallas_call(
        functools.partial(_gdn_in_kernel, tm=tm, tiles_per_batch=lp // tm),
        grid=(m // tm,),
        in_specs=[
            pl.BlockSpec((tm, D_MODEL), row),
            pl.BlockSpec((1, D_MODEL), const),
            pl.BlockSpec((D_MODEL, 4 * GDN_WIDTH), const),
            pl.BlockSpec((D_MODEL, LANES), const),
            pl.BlockSpec((CONV_WIDTH, 3 * GDN_WIDTH), const),
            pl.BlockSpec((SUBLANES, LANES), const),
        ],
        out_specs=[wide, wide, wide, wide, pl.BlockSpec((tm, LANES), row)],
        out_shape=[jax.ShapeDtypeStruct((m, GDN_WIDTH), F32)] * 4 + [jax.ShapeDtypeStruct((m, LANES), F32)],
        scratch_shapes=[
            pltpu.VMEM((3, SUBLANES, GDN_WIDTH), F32),
            pltpu.VMEM((tm + SUBLANES, GDN_WIDTH), F32),
        ],
        compiler_params=_params(("arbitrary",)),
        name="gdn_in",
    )(h, ng, w_main, w_ab, conv_w, ab_par)


def _unit_lower_inverse(n_strict, blk_same):
    c = n_strict.shape[0]
    eye = (lax.broadcasted_iota(jnp.int32, (c, c), 0) == lax.broadcasted_iota(jnp.int32, (c, c), 1)).astype(F32)
    nd = jnp.where(blk_same, n_strict, 0.0)
    no = n_strict - nd
    nd2 = _dot3(nd, nd)
    nd4 = _dot3(nd2, nd2)
    nd8 = _dot3(nd4, nd4)
    dinv = _dot3(_dot3(eye - nd, eye + nd2), eye + nd4)
    dinv = _dot3(dinv, eye + nd8)
    mm = _dot3(dinv, no)
    m2 = _dot3(mm, mm)
    m4 = _dot3(m2, m2)
    outer = _dot3(_dot3(eye - mm, eye + m2), eye + m4)
    return _dot3(outer, dinv)


def _gdn_prep_kernel(q_ref, k_ref, v_ref, gb_ref,
                     ub_ref, wc_ref, qd_ref, kdt_ref, at_ref, egl_ref):
    c = GCHUNK
    ri = lax.broadcasted_iota(jnp.int32, (c, c), 0)
    ci = lax.broadcasted_iota(jnp.int32, (c, c), 1)
    incl = ri >= ci
    strict = ri > ci
    blk_same = (ri // 16) == (ci // 16)
    tril = incl.astype(BF16)
    scale = GDN_HEAD_DIM ** -0.5
    for hd in range(GDN_HEADS):
        sl = slice(hd * GDN_HEAD_DIM, (hd + 1) * GDN_HEAD_DIM)
        q = q_ref[:, sl] * scale
        k = k_ref[:, sl]
        v = v_ref[:, sl]
        gmat = jnp.broadcast_to(gb_ref[:, hd:hd + 1], (c, c))
        bmat = jnp.broadcast_to(gb_ref[:, GDN_HEADS + hd:GDN_HEADS + hd + 1], (c, c))
        gcol = _dot_exact_lhs(tril, gmat)
        grow = gcol.T
        decay = jnp.where(incl, jnp.exp(jnp.where(incl, gcol - grow, 0.0)), 0.0)
        kb = k.astype(BF16)
        kk = _dot_nt(kb, kb)
        n_strict = jnp.where(strict, bmat * decay * kk, 0.0)
        t = _unit_lower_inverse(n_strict, blk_same)
        eg = jnp.exp(gcol)
        rhs = jnp.concatenate([bmat * v, (bmat * eg) * k], axis=1).astype(BF16)
        sol = _dot(t.astype(BF16), rhs)
        ub_ref[:, sl] = sol[:, :GDN_HEAD_DIM]
        wc_ref[:, sl] = sol[:, GDN_HEAD_DIM:].astype(BF16)
        at_ref[:, sl] = (_dot_nt(q.astype(BF16), kb) * decay).astype(BF16)
        qd_ref[:, sl] = (q * eg).astype(BF16)
        glast = gcol[c - 1:c, :]
        kdt_ref[:, sl] = (k * jnp.exp(glast - gcol)).T.astype(BF16)
        egl_ref[:, sl] = jnp.broadcast_to(jnp.exp(glast), (SUBLANES, GDN_HEAD_DIM))


def _gdn_prep(q, k, v, gb):
    m = q.shape[0]
    nchunks = m // GCHUNK
    row = lambda r: (r, 0)
    wide = pl.BlockSpec((GCHUNK, GDN_WIDTH), row)
    f32w = jax.ShapeDtypeStruct((m, GDN_WIDTH), F32)
    b16w = jax.ShapeDtypeStruct((m, GDN_WIDTH), BF16)
    return pl.pallas_call(
        _gdn_prep_kernel,
        grid=(nchunks,),
        in_specs=[wide, wide, wide, pl.BlockSpec((GCHUNK, LANES), row)],
        out_specs=[wide, wide, wide, wide, wide, pl.BlockSpec((SUBLANES, GDN_WIDTH), row)],
        out_shape=[f32w, b16w, b16w, b16w, b16w,
                   jax.ShapeDtypeStruct((nchunks * SUBLANES, GDN_WIDTH), F32)],
        compiler_params=_params(("parallel",)),
        name="gdn_prep",
    )(q, k, v, gb)


def _gdn_scan_kernel(ub_ref, wc_ref, qd_ref, kdt_ref, at_ref, egl_ref, gate_ref, h_ref, og_ref, wo_ref,
                     out_ref, s_ref, o_ref):
    @pl.when(pl.program_id(1) == 0)
    def _():
        s_ref[...] = jnp.zeros_like(s_ref)

    for hd in range(GDN_HEADS):
        sl = slice(hd * GDN_HEAD_DIM, (hd + 1) * GDN_HEAD_DIM)
        s = s_ref[hd]
        sb = s.astype(BF16)
        u = ub_ref[:, sl] - _dot(wc_ref[:, sl], sb)
        ub = u.astype(BF16)
        o = _dot(qd_ref[:, sl], sb) + _dot(at_ref[:, sl], ub)
        s_ref[hd] = s * egl_ref[0:1, sl] + _dot(kdt_ref[:, sl], ub)
        o = o * lax.rsqrt(jnp.mean(o * o, axis=-1, keepdims=True) + EPS) * og_ref[...]
        gate = gate_ref[:, sl]
        o_ref[:, sl] = (o * (gate * _sigmoid(gate))).astype(BF16)
    out_ref[...] = h_ref[...] + _dot(o_ref[...], wo_ref[...])


def _gdn_scan(ub, wc, qd, kdt, at, egl, gate, h, og, wo, *, batch):
    m = h.shape[0]
    nch = m // GCHUNK // batch
    row = lambda b, c: (b * nch + c, 0)
    const = lambda b, c: (0, 0)
    wide = pl.BlockSpec((GCHUNK, GDN_WIDTH), row)
    return pl.pallas_call(
        _gdn_scan_kernel,
        grid=(batch, nch),
        in_specs=[wide, wide, wide, wide, wide,
                  pl.BlockSpec((SUBLANES, GDN_WIDTH), row),
                  wide,
                  pl.BlockSpec((GCHUNK, D_MODEL), row),
                  pl.BlockSpec((1, GDN_HEAD_DIM), const),
                  pl.BlockSpec((GDN_WIDTH, D_MODEL), const)],
        out_specs=pl.BlockSpec((GCHUNK, D_MODEL), row),
        out_shape=jax.ShapeDtypeStruct((m, D_MODEL), F32),
        scratch_shapes=[pltpu.VMEM((GDN_HEADS, GDN_HEAD_DIM, GDN_HEAD_DIM), F32),
                        pltpu.VMEM((GCHUNK, GDN_WIDTH), BF16)],
        compiler_params=_params(("arbitrary", "arbitrary")),
        name="gdn_scan",
    )(ub, wc, qd, kdt, at, egl, gate, h, og, wo)


def _ffn_kernel(x_ref, g_ref, wg_ref, wu_ref, wd_ref, out_ref, xn_ref, acc_ref):
    j = pl.program_id(1)

    @pl.when(j == 0)
    def _():
        xn_ref[...] = _rms(x_ref[...], g_ref[...]).astype(BF16)
        acc_ref[...] = x_ref[...]

    xn = xn_ref[...]
    gate = _dot(xn, wg_ref[...])
    up = _dot(xn, wu_ref[...])
    act = (gate * _sigmoid(gate) * up).astype(BF16)
    acc_ref[...] += _dot(act, wd_ref[...])

    @pl.when(j == pl.num_programs(1) - 1)
    def _():
        out_ref[...] = acc_ref[...]


def _ffn(x, g, w_gu, w_d, *, tm, tf):
    m = x.shape[0]
    nf = D_FF // tf
    return pl.pallas_call(
        _ffn_kernel,
        grid=(m // tm, nf),
        in_specs=[
            pl.BlockSpec((tm, D_MODEL), lambda r, j: (r, 0)),
            pl.BlockSpec((1, D_MODEL), lambda r, j: (0, 0)),
            pl.BlockSpec((D_MODEL, tf), lambda r, j: (0, j)),
            pl.BlockSpec((D_MODEL, tf), lambda r, j: (0, nf + j)),
            pl.BlockSpec((tf, D_MODEL), lambda r, j: (j, 0)),
        ],
        out_specs=pl.BlockSpec((tm, D_MODEL), lambda r, j: (r, 0)),
        out_shape=jax.ShapeDtypeStruct((m, D_MODEL), F32),
        scratch_shapes=[pltpu.VMEM((tm, D_MODEL), BF16), pltpu.VMEM((tm, D_MODEL), F32)],
        compiler_params=_params(("parallel", "arbitrary")),
        name="ffn",
    )(x, g, w_gu, w_gu, w_d)


def _qkv_kernel(x_ref, gkv_ref, gq_ref, wkv_ref, wq_ref, q_ref, k_ref, v_ref):
    x = x_ref[...]
    inv = lax.rsqrt(jnp.mean(x * x, axis=-1, keepdims=True) + EPS)
    xkv = (x * inv * gkv_ref[...]).astype(BF16)
    xq = (x * inv * gq_ref[...]).astype(BF16)
    k_ref[...] = _dot(xkv, wkv_ref[:, :SB_WIDTH]).astype(BF16)
    v_ref[...] = _dot(xkv, wkv_ref[:, SB_WIDTH:]).astype(BF16)
    q_ref[...] = (_dot(xq, wq_ref[...]) * (SB_HEAD_DIM ** -0.5)).astype(BF16)


def _qkv(x, gkv, gq, wkv, wq, *, tm):
    m = x.shape[0]
    row = lambda r: (r, 0)
    const = lambda r: (0, 0)
    out = pl.BlockSpec((tm, SB_WIDTH), row)
    return pl.pallas_call(
        _qkv_kernel,
        grid=(m // tm,),
        in_specs=[pl.BlockSpec((tm, D_MODEL), row),
                  pl.BlockSpec((1, D_MODEL), const),
                  pl.BlockSpec((1, D_MODEL), const),
                  pl.BlockSpec((D_MODEL, 2 * SB_WIDTH), const),
                  pl.BlockSpec((D_MODEL, SB_WIDTH), const)],
        out_specs=[out, out, out],
        out_shape=[jax.ShapeDtypeStruct((m, SB_WIDTH), BF16)] * 3,
        compiler_params=_params(("parallel",)),
        name="sb_qkv",
    )(x, gkv, gq, wkv, wq)


def _sb_kernel(q_ref, k_ref, v_ref, o_ref):
    tq = SB_BLOCK
    iq = pl.program_id(2)
    lane = lax.broadcasted_iota(jnp.int32, (tq, LANES), 1)
    first = lane < SB_HEAD_DIM
    q2 = q_ref[...]
    zero = jnp.zeros_like(q2)
    qs = (jnp.where(first, q2, zero), jnp.where(first, zero, q2))
    ri = lax.broadcasted_iota(jnp.int32, (tq, tq), 0)
    ci = lax.broadcasted_iota(jnp.int32, (tq, tq), 1)
    upper = (ri > ci).astype(BF16)
    ones = jnp.ones((tq, LANES), BF16)
    qpos = iq * tq + ri

    def cond(carry):
        j, ca, cb, _, _ = carry
        return jnp.logical_and(j >= 0, jnp.minimum(jnp.min(ca), jnp.min(cb)) < SB_SKIP)

    def body(carry):
        j, ca, cb, oa, ob = carry
        start = pl.multiple_of(j * tq, tq)
        kj = k_ref[pl.ds(start, tq), :]
        vj = v_ref[pl.ds(start, tq), :]
        kpos = j * tq + ci
        vis = jnp.logical_and(kpos < qpos, kpos >= FRONT)
        new = []
        for qh, c, o in ((qs[0], ca, oa), (qs[1], cb, ob)):
            z = _dot_nt(qh, kj)
            sp = _softplus(z)
            lb = z - sp
            spm = jnp.where(vis, sp, 0.0)
            hi, lo = _split(spm)
            rc = _dot(hi, upper) + _dot(lo, upper)
            tot = _dot(hi, ones) + _dot(lo, ones)
            w = jnp.where(vis, jnp.exp(lb - rc - c), 0.0)
            new.append((c + tot, o + _dot(w.astype(BF16), vj)))
        return j - 1, new[0][0], new[1][0], new[0][1], new[1][1]

    zf = jnp.zeros((tq, LANES), F32)
    _, _, _, oa, ob = lax.while_loop(cond, body, (iq, zf, zf, zf, zf))
    o_ref[...] = jnp.where(first, oa, ob).astype(BF16)


def _sb_attention(q, k, v, *, batch, lp):
    m = q.shape[0]
    nq = lp // SB_BLOCK
    pairs = SB_WIDTH // LANES
    return pl.pallas_call(
        _sb_kernel,
        grid=(batch, pairs, nq),
        in_specs=[pl.BlockSpec((SB_BLOCK, LANES), lambda b, p, i: (b * nq + i, p)),
                  pl.BlockSpec((lp, LANES), lambda b, p, i: (b, p)),
                  pl.BlockSpec((lp, LANES), lambda b, p, i: (b, p))],
        out_specs=pl.BlockSpec((SB_BLOCK, LANES), lambda b, p, i: (b * nq + i, p)),
        out_shape=jax.ShapeDtypeStruct((m, SB_WIDTH), BF16),
        compiler_params=_params(("parallel", "parallel", "arbitrary")),
        name="sb_attn",
    )(q, k, v)


def _proj_res_kernel(x_ref, a_ref, w_ref, out_ref):
    out_ref[...] = x_ref[...] + _dot(a_ref[...], w_ref[...])


def _proj_res(x, a, w, *, tm):
    m = x.shape[0]
    row = lambda r: (r, 0)
    return pl.pallas_call(
        _proj_res_kernel,
        grid=(m // tm,),
        in_specs=[pl.BlockSpec((tm, D_MODEL), row),
                  pl.BlockSpec((tm, a.shape[1]), row),
                  pl.BlockSpec(w.shape, lambda r: (0, 0))],
        out_specs=pl.BlockSpec((tm, D_MODEL), row),
        out_shape=jax.ShapeDtypeStruct((m, D_MODEL), F32),
        compiler_params=_params(("parallel",)),
        name="sb_out",
    )(x, a, w)


def _final_norm_kernel(x_ref, g_ref, out_ref):
    out_ref[0] = _rms(x_ref[...], g_ref[...])


def _final_norm(x, g, *, batch, lp, seq):
    nb = lp // GCHUNK
    return pl.pallas_call(
        _final_norm_kernel,
        grid=(batch, seq // GCHUNK),
        in_specs=[pl.BlockSpec((GCHUNK, D_MODEL), lambda b, r: (b * nb + 1 + r, 0)),
                  pl.BlockSpec((1, D_MODEL), lambda b, r: (0, 0))],
        out_specs=pl.BlockSpec((1, GCHUNK, D_MODEL), lambda b, r: (b, r, 0)),
        out_shape=jax.ShapeDtypeStruct((batch, seq, D_MODEL), F32),
        compiler_params=_params(("parallel", "parallel")),
        name="final_norm",
    )(x, g)


def _row_tile(lp):
    for tm in (640, 512, 384, 256, 128):
        if lp % tm == 0:
            return tm
    raise ValueError(f"padded length {lp} is not a multiple of {GCHUNK}")


def kernel(x, meta_tokens, gdn_norm_g, gdn_w_in, gdn_conv_w, gdn_a_log, gdn_dt_bias, gdn_onorm_g, gdn_w_out,
           kv_norm_g, w_kv, sb_norm_g, sb_w_q, sb_w_o, ffn_norm_g, ffn_w_gate_up, ffn_w_down, final_norm_g):
    batch, seq, _ = x.shape
    assert seq % GCHUNK == 0 and gdn_w_in.shape[0] == 1 and sb_w_q.shape[0] == 1
    lp = GCHUNK + seq
    tm = _row_tile(lp)
    tf = D_FF // 2

    head = jnp.concatenate([jnp.zeros((FRONT, D_MODEL), x.dtype), meta_tokens.astype(x.dtype)], axis=0)
    h = jnp.concatenate([jnp.broadcast_to(head[None], (batch, GCHUNK, D_MODEL)), x], axis=1)
    h = h.reshape(batch * lp, D_MODEL)

    w_in = gdn_w_in[0]
    w_main = w_in[:, :4 * GDN_WIDTH].astype(BF16)
    w_ab = jnp.pad(w_in[:, 4 * GDN_WIDTH:], ((0, 0), (0, LANES - 2 * GDN_HEADS))).astype(BF16)
    ab_par = jnp.zeros((SUBLANES, LANES), F32)
    ab_par = ab_par.at[0, :GDN_HEADS].set(gdn_a_log[0]).at[1, :GDN_HEADS].set(gdn_dt_bias[0])

    q, k, v, gate, gb = _gdn_in(h, gdn_norm_g[0][None], w_main, w_ab, gdn_conv_w[0], ab_par, tm=tm, lp=lp)
    ub, wc, qd, kdt, at, egl = _gdn_prep(q, k, v, gb)
    h = _gdn_scan(ub, wc, qd, kdt, at, egl, gate, h, gdn_onorm_g[0][None], gdn_w_out[0].astype(BF16), batch=batch)
    h = _ffn(h, ffn_norm_g[0][None], ffn_w_gate_up[0].astype(BF16), ffn_w_down[0].astype(BF16), tm=tm, tf=tf)

    sq, sk, sv = _qkv(h, kv_norm_g[None], sb_norm_g[0][None], w_kv.astype(BF16), sb_w_q[0].astype(BF16), tm=tm)
    o = _sb_attention(sq, sk, sv, batch=batch, lp=lp)
    h = _proj_res(h, o, sb_w_o[0].astype(BF16), tm=tm)
    h = _ffn(h, ffn_norm_g[1][None], ffn_w_gate_up[1].astype(BF16), ffn_w_down[1].astype(BF16), tm=tm, tf=tf)
    return _final_norm(h, final_norm_g[None], batch=batch, lp=lp, seq=seq)
```

```python
import functools

import jax
import jax.numpy as jnp
from jax import lax
from jax.experimental import pallas as pl
from jax.experimental.pallas import tpu as pltpu

F32 = jnp.float32
BF16 = jnp.bfloat16

D_MODEL = 1024
N_META = 16
GDN_HEADS = 8
GDN_HEAD_DIM = 128
GDN_WIDTH = GDN_HEADS * GDN_HEAD_DIM
CONV_WIDTH = 4
SB_HEADS = 16
SB_HEAD_DIM = 64
SB_WIDTH = SB_HEADS * SB_HEAD_DIM
D_FF = 2816
EPS = 1e-6

LANES = 128
SUBLANES = 8
GCHUNK = 128
FRONT = GCHUNK - N_META
SB_BLOCK = 128
SB_PAIRS = SB_WIDTH // LANES
SB_SKIP = 100.0
SB_MASKED = -1e30
VMEM_LIMIT = 56 * 1024 * 1024


def _dot(a, b):
    return lax.dot_general(a, b, (((1,), (0,)), ((), ())), preferred_element_type=F32)


def _dot_nt(a, b):
    return lax.dot_general(a, b, (((1,), (1,)), ((), ())), preferred_element_type=F32)


def _split(a):
    hi = a.astype(BF16)
    lo = (a - hi.astype(F32)).astype(BF16)
    return hi, lo


def _dot3(a, b):
    ah, al = _split(a)
    bh, bl = _split(b)
    return _dot(ah, bh) + (_dot(ah, bl) + _dot(al, bh))


def _dot_exact_lhs(m_bf16, b):
    bh, bl = _split(b)
    return _dot(m_bf16, bh) + _dot(m_bf16, bl)


def _dot_exact_rhs(a, m_bf16):
    ah, al = _split(a)
    return _dot(ah, m_bf16) + _dot(al, m_bf16)


def _rms(x, g):
    return x * lax.rsqrt(jnp.mean(x * x, axis=-1, keepdims=True) + EPS) * g


def _sigmoid(x):
    return 1.0 / (1.0 + jnp.exp(-x))


def _softplus(x):
    return jnp.maximum(x, 0.0) + jnp.log(1.0 + jnp.exp(-jnp.abs(x)))


def _params(sem):
    return pltpu.CompilerParams(dimension_semantics=sem, vmem_limit_bytes=VMEM_LIMIT)


def _gdn_in_kernel(h_ref, ng_ref, w_ref, wab_ref, cw_ref, ab_ref,
                   q_ref, k_ref, v_ref, gate_ref, gb_ref,
                   carry_ref, cbuf_ref, *, tm, tiles_per_batch):
    tb = pl.program_id(0) % tiles_per_batch

    @pl.when(tb == 0)
    def _():
        carry_ref[...] = jnp.zeros_like(carry_ref)

    xn = _rms(h_ref[...], ng_ref[...]).astype(BF16)
    outs = (q_ref, k_ref, v_ref)
    for part in range(3):
        cols = slice(part * GDN_WIDTH, (part + 1) * GDN_WIDTH)
        p = _dot(xn, w_ref[:, cols])
        cbuf_ref[0:SUBLANES, :] = carry_ref[part]
        cbuf_ref[SUBLANES:, :] = p
        carry_ref[part] = p[tm - SUBLANES:, :]
        cw = cw_ref[:, cols]
        y = p * cw[CONV_WIDTH - 1:CONV_WIDTH, :]
        for i in range(1, CONV_WIDTH):
            y = y + cbuf_ref[pl.ds(SUBLANES - i, tm), :] * cw[CONV_WIDTH - 1 - i:CONV_WIDTH - i, :]
        y = y * _sigmoid(y)
        if part < 2:
            for hd in range(GDN_HEADS):
                sl = slice(hd * GDN_HEAD_DIM, (hd + 1) * GDN_HEAD_DIM)
                seg = y[:, sl]
                outs[part][:, sl] = seg * lax.rsqrt(jnp.sum(seg * seg, axis=-1, keepdims=True) + EPS)
        else:
            v_ref[...] = y
    gate_ref[...] = _dot(xn, w_ref[:, 3 * GDN_WIDTH:4 * GDN_WIDTH])

    ab = _dot(xn, wab_ref[...])
    a_log = ab_ref[0:1, :]
    dt_bias = ab_ref[1:2, :]
    g = -jnp.exp(a_log) * _softplus(ab + dt_bias)
    beta = _sigmoid(ab)
    lane = lax.broadcasted_iota(jnp.int32, ab.shape, 1)
    pos = tb * tm + lax.broadcasted_iota(jnp.int32, ab.shape, 0)
    gb_ref[...] = jnp.where(pos >= FRONT, jnp.where(lane < GDN_HEADS, g, beta), 0.0)


def _gdn_in(h, ng, w_main, w_ab, conv_w, ab_par, *, tm, lp):
    m = h.shape[0]
    row = lambda r: (r, 0)
    const = lambda r: (0, 0)
    wide = pl.BlockSpec((tm, GDN_WIDTH), row)
    return pl.pallas_call(
        functools.partial(_gdn_in_kernel, tm=tm, tiles_per_batch=lp // tm),
        grid=(m // tm,),
        in_specs=[
            pl.BlockSpec((tm, D_MODEL), row),
            pl.BlockSpec((1, D_MODEL), const),
            pl.BlockSpec((D_MODEL, 4 * GDN_WIDTH), const),
            pl.BlockSpec((D_MODEL, LANES), const),
            pl.BlockSpec((CONV_WIDTH, 3 * GDN_WIDTH), const),
            pl.BlockSpec((SUBLANES, LANES), const),
        ],
        out_specs=[wide, wide, wide, wide, pl.BlockSpec((tm, LANES), row)],
        out_shape=[jax.ShapeDtypeStruct((m, GDN_WIDTH), F32)] * 4 + [jax.ShapeDtypeStruct((m, LANES), F32)],
        scratch_shapes=[
            pltpu.VMEM((3, SUBLANES, GDN_WIDTH), F32),
            pltpu.VMEM((tm + SUBLANES, GDN_WIDTH), F32),
        ],
        compiler_params=_params(("arbitrary",)),
        name="gdn_in",
    )(h, ng, w_main, w_ab, conv_w, ab_par)


def _unit_lower_inverse(n_strict, blk_same):
    c = n_strict.shape[0]
    eye = (lax.broadcasted_iota(jnp.int32, (c, c), 0) == lax.broadcasted_iota(jnp.int32, (c, c), 1)).astype(F32)
    nd = jnp.where(blk_same, n_strict, 0.0)
    no = n_strict - nd
    nd2 = _dot3(nd, nd)
    nd4 = _dot3(nd2, nd2)
    nd8 = _dot3(nd4, nd4)
    dinv = _dot3(_dot3(eye - nd, eye + nd2), eye + nd4)
    dinv = _dot3(dinv, eye + nd8)
    mm = _dot3(dinv, no)
    m2 = _dot3(mm, mm)
    m4 = _dot3(m2, m2)
    outer = _dot3(_dot3(eye - mm, eye + m2), eye + m4)
    return _dot3(outer, dinv)


def _gdn_prep_kernel(q_ref, k_ref, v_ref, gb_ref,
                     ub_ref, wc_ref, qd_ref, kdt_ref, at_ref, egl_ref):
    c = GCHUNK
    ri = lax.broadcasted_iota(jnp.int32, (c, c), 0)
    ci = lax.broadcasted_iota(jnp.int32, (c, c), 1)
    incl = ri >= ci
    strict = ri > ci
    blk_same = (ri // 16) == (ci // 16)
    tril = incl.astype(BF16)
    scale = GDN_HEAD_DIM ** -0.5
    for hd in range(GDN_HEADS):
        sl = slice(hd * GDN_HEAD_DIM, (hd + 1) * GDN_HEAD_DIM)
        q = q_ref[:, sl] * scale
        k = k_ref[:, sl]
        v = v_ref[:, sl]
        gmat = jnp.broadcast_to(gb_ref[:, hd:hd + 1], (c, c))
        bmat = jnp.broadcast_to(gb_ref[:, GDN_HEADS + hd:GDN_HEADS + hd + 1], (c, c))
        gcol = _dot_exact_lhs(tril, gmat)
        grow = gcol.T
        decay = jnp.where(incl, jnp.exp(jnp.where(incl, gcol - grow, 0.0)), 0.0)
        kb = k.astype(BF16)
        kk = _dot_nt(kb, kb)
        n_strict = jnp.where(strict, bmat * decay * kk, 0.0)
        t = _unit_lower_inverse(n_strict, blk_same)
        eg = jnp.exp(gcol)
        rhs = jnp.concatenate([bmat * v, (bmat * eg) * k], axis=1).astype(BF16)
        sol = _dot(t.astype(BF16), rhs)
        ub_ref[:, sl] = sol[:, :GDN_HEAD_DIM]
        wc_ref[:, sl] = sol[:, GDN_HEAD_DIM:].astype(BF16)
        at_ref[:, sl] = (_dot_nt(q.astype(BF16), kb) * decay).astype(BF16)
        qd_ref[:, sl] = (q * eg).astype(BF16)
        glast = gcol[c - 1:c, :]
        kdt_ref[:, sl] = (k * jnp.exp(glast - gcol)).T.astype(BF16)
        egl_ref[:, sl] = jnp.broadcast_to(jnp.exp(glast), (SUBLANES, GDN_HEAD_DIM))


def _gdn_prep(q, k, v, gb):
    m = q.shape[0]
    nchunks = m // GCHUNK
    row = lambda r: (r, 0)
    wide = pl.BlockSpec((GCHUNK, GDN_WIDTH), row)
    f32w = jax.ShapeDtypeStruct((m, GDN_WIDTH), F32)
    b16w = jax.ShapeDtypeStruct((m, GDN_WIDTH), BF16)
    return pl.pallas_call(
        _gdn_prep_kernel,
        grid=(nchunks,),
        in_specs=[wide, wide, wide, pl.BlockSpec((GCHUNK, LANES), row)],
        out_specs=[wide, wide, wide, wide, wide, pl.BlockSpec((SUBLANES, GDN_WIDTH), row)],
        out_shape=[f32w, b16w, b16w, b16w, b16w,
                   jax.ShapeDtypeStruct((nchunks * SUBLANES, GDN_WIDTH), F32)],
        compiler_params=_params(("parallel",)),
        name="gdn_prep",
    )(q, k, v, gb)


def _gdn_scan_kernel(ub_ref, wc_ref, qd_ref, kdt_ref, at_ref, egl_ref, gate_ref, h_ref, og_ref, wo_ref,
                     out_ref, s_ref, o_ref):
    @pl.when(pl.program_id(1) == 0)
    def _():
        s_ref[...] = jnp.zeros_like(s_ref)

    for hd in range(GDN_HEADS):
        sl = slice(hd * GDN_HEAD_DIM, (hd + 1) * GDN_HEAD_DIM)
        s = s_ref[hd]
        sb = s.astype(BF16)
        u = ub_ref[:, sl] - _dot(wc_ref[:, sl], sb)
        ub = u.astype(BF16)
        o = _dot(qd_ref[:, sl], sb) + _dot(at_ref[:, sl], ub)
        s_ref[hd] = s * egl_ref[0:1, sl] + _dot(kdt_ref[:, sl], ub)
        o = o * lax.rsqrt(jnp.mean(o * o, axis=-1, keepdims=True) + EPS) * og_ref[...]
        gate = gate_ref[:, sl]
        o_ref[:, sl] = (o * (gate * _sigmoid(gate))).astype(BF16)
    out_ref[...] = h_ref[...] + _dot(o_ref[...], wo_ref[...])


def _gdn_scan(ub, wc, qd, kdt, at, egl, gate, h, og, wo, *, batch):
    m = h.shape[0]
    nch = m // GCHUNK // batch
    row = lambda b, c: (b * nch + c, 0)
    const = lambda b, c: (0, 0)
    wide = pl.BlockSpec((GCHUNK, GDN_WIDTH), row)
    return pl.pallas_call(
        _gdn_scan_kernel,
        grid=(batch, nch),
        in_specs=[wide, wide, wide, wide, wide,
                  pl.BlockSpec((SUBLANES, GDN_WIDTH), row),
                  wide,
                  pl.BlockSpec((GCHUNK, D_MODEL), row),
                  pl.BlockSpec((1, GDN_HEAD_DIM), const),
                  pl.BlockSpec((GDN_WIDTH, D_MODEL), const)],
        out_specs=pl.BlockSpec((GCHUNK, D_MODEL), row),
        out_shape=jax.ShapeDtypeStruct((m, D_MODEL), F32),
        scratch_shapes=[pltpu.VMEM((GDN_HEADS, GDN_HEAD_DIM, GDN_HEAD_DIM), F32),
                        pltpu.VMEM((GCHUNK, GDN_WIDTH), BF16)],
        compiler_params=_params(("arbitrary", "arbitrary")),
        name="gdn_scan",
    )(ub, wc, qd, kdt, at, egl, gate, h, og, wo)


def _ffn_kernel(x_ref, g_ref, wg_ref, wu_ref, wd_ref, out_ref, xn_ref, acc_ref):
    j = pl.program_id(1)

    @pl.when(j == 0)
    def _():
        xn_ref[...] = _rms(x_ref[...], g_ref[...]).astype(BF16)
        acc_ref[...] = x_ref[...]

    xn = xn_ref[...]
    gate = _dot(xn, wg_ref[...])
    up = _dot(xn, wu_ref[...])
    act = (gate * _sigmoid(gate) * up).astype(BF16)
    acc_ref[...] += _dot(act, wd_ref[...])

    @pl.when(j == pl.num_programs(1) - 1)
    def _():
        out_ref[...] = acc_ref[...]


def _ffn(x, g, w_gu, w_d, *, tm, tf):
    m = x.shape[0]
    nf = D_FF // tf
    return pl.pallas_call(
        _ffn_kernel,
        grid=(m // tm, nf),
        in_specs=[
            pl.BlockSpec((tm, D_MODEL), lambda r, j: (r, 0)),
            pl.BlockSpec((1, D_MODEL), lambda r, j: (0, 0)),
            pl.BlockSpec((D_MODEL, tf), lambda r, j: (0, j)),
            pl.BlockSpec((D_MODEL, tf), lambda r, j: (0, nf + j)),
            pl.BlockSpec((tf, D_MODEL), lambda r, j: (j, 0)),
        ],
        out_specs=pl.BlockSpec((tm, D_MODEL), lambda r, j: (r, 0)),
        out_shape=jax.ShapeDtypeStruct((m, D_MODEL), F32),
        scratch_shapes=[pltpu.VMEM((tm, D_MODEL), BF16), pltpu.VMEM((tm, D_MODEL), F32)],
        compiler_params=_params(("parallel", "arbitrary")),
        name="ffn",
    )(x, g, w_gu, w_gu, w_d)


def _qkv_kernel(x_ref, gkv_ref, gq_ref, wkv_ref, wq_ref, q_ref, k_ref, v_ref):
    x = x_ref[...]
    inv = lax.rsqrt(jnp.mean(x * x, axis=-1, keepdims=True) + EPS)
    xkv = (x * inv * gkv_ref[...]).astype(BF16)
    xq = (x * inv * gq_ref[...]).astype(BF16)
    k_ref[...] = _dot(xkv, wkv_ref[:, :SB_WIDTH]).astype(BF16)
    v_ref[...] = _dot(xkv, wkv_ref[:, SB_WIDTH:]).astype(BF16)
    q_ref[...] = (_dot(xq, wq_ref[...]) * (SB_HEAD_DIM ** -0.5)).astype(BF16)


def _qkv(x, gkv, gq, wkv, wq, *, tm):
    m = x.shape[0]
    row = lambda r: (r, 0)
    const = lambda r: (0, 0)
    out = pl.BlockSpec((tm, SB_WIDTH), row)
    return pl.pallas_call(
        _qkv_kernel,
        grid=(m // tm,),
        in_specs=[pl.BlockSpec((tm, D_MODEL), row),
                  pl.BlockSpec((1, D_MODEL), const),
                  pl.BlockSpec((1, D_MODEL), const),
                  pl.BlockSpec((D_MODEL, 2 * SB_WIDTH), const),
                  pl.BlockSpec((D_MODEL, SB_WIDTH), const)],
        out_specs=[out, out, out],
        out_shape=[jax.ShapeDtypeStruct((m, SB_WIDTH), BF16)] * 3,
        compiler_params=_params(("parallel",)),
        name="sb_qkv",
    )(x, gkv, gq, wkv, wq)


def _sb_kernel(q_ref, k_ref, v_ref, h_ref, wo_ref, out_ref, qm_ref, c_ref, o_ref, obf_ref):
    tq = SB_BLOCK
    iq = pl.program_id(1)
    first = lax.broadcasted_iota(jnp.int32, (tq, LANES), 1) < SB_HEAD_DIM
    ri = lax.broadcasted_iota(jnp.int32, (tq, tq), 0)
    ci = lax.broadcasted_iota(jnp.int32, (tq, tq), 1)
    upper_ext = jnp.concatenate([(ri > ci).astype(BF16), jnp.ones((tq, LANES), BF16)], axis=1)

    for p in range(SB_PAIRS):
        q2 = q_ref[:, p * LANES:(p + 1) * LANES]
        zero = jnp.zeros_like(q2)
        qm_ref[p, 0:tq, :] = jnp.where(first, q2, zero)
        qm_ref[p, tq:, :] = jnp.where(first, zero, q2)
    c_ref[...] = jnp.zeros_like(c_ref)
    o_ref[...] = jnp.zeros_like(o_ref)

    def visit(j, bias):
        start = pl.multiple_of(j * tq, tq)
        pairs = range(SB_PAIRS)
        cols = [slice(p * LANES, (p + 1) * LANES) for p in pairs]
        zs = [_dot_nt(qm_ref[p], k_ref[pl.ds(start, tq), cols[p]]) + bias for p in pairs]
        sps = [_softplus(z) for z in zs]
        parts = [_split(sp) for sp in sps]
        accs = [_dot(hi, upper_ext) + _dot(lo, upper_ext) for hi, lo in parts]
        ws = []
        cmin = None
        for p in pairs:
            c = c_ref[p]
            ws.append(jnp.exp(zs[p] - sps[p] - accs[p][:, :tq] - c).astype(BF16))
            c = c + accs[p][:, tq:]
            c_ref[p] = c
            cmin = c if cmin is None else jnp.minimum(cmin, c)
        for p in pairs:
            vj = v_ref[pl.ds(start, tq), cols[p]]
            zero = jnp.zeros_like(vj)
            v2 = jnp.concatenate([jnp.where(first, vj, zero), jnp.where(first, zero, vj)], axis=0)
            o_ref[p] += _dot(jnp.concatenate([ws[p][:tq], ws[p][tq:]], axis=1), v2)
        return jnp.min(cmin)

    qpos = iq * tq + ri
    kpos = iq * tq + ci
    diag = jnp.where(jnp.logical_and(kpos < qpos, kpos >= FRONT), 0.0, SB_MASKED)
    lowest = visit(iq, jnp.concatenate([diag, diag], axis=0))

    def cond(carry):
        j, low = carry
        return jnp.logical_and(j >= 0, low < SB_SKIP)

    def body(carry):
        j, _ = carry
        kp = j * tq + lax.broadcasted_iota(jnp.int32, (1, tq), 1)
        return j - 1, visit(j, jnp.where(kp >= FRONT, 0.0, SB_MASKED))

    lax.while_loop(cond, body, (iq - 1, lowest))

    for p in range(SB_PAIRS):
        obf_ref[:, p * LANES:(p + 1) * LANES] = o_ref[p].astype(BF16)
    out_ref[...] = h_ref[...] + _dot(obf_ref[...], wo_ref[...])


def _sb_layer(h, q, k, v, wo, *, batch, lp):
    m = q.shape[0]
    nq = lp // SB_BLOCK
    tq = SB_BLOCK
    row = lambda b, i: (b * nq + i, 0)
    whole = lambda b, i: (b, 0)
    return pl.pallas_call(
        _sb_kernel,
        grid=(batch, nq),
        in_specs=[pl.BlockSpec((tq, SB_WIDTH), row),
                  pl.BlockSpec((lp, SB_WIDTH), whole, pipeline_mode=pl.Buffered(1)),
                  pl.BlockSpec((lp, SB_WIDTH), whole, pipeline_mode=pl.Buffered(1)),
                  pl.BlockSpec((tq, D_MODEL), row),
                  pl.BlockSpec((SB_WIDTH, D_MODEL), lambda b, i: (0, 0))],
        out_specs=pl.BlockSpec((tq, D_MODEL), row),
        out_shape=jax.ShapeDtypeStruct((m, D_MODEL), F32),
        scratch_shapes=[pltpu.VMEM((SB_PAIRS, 2 * tq, LANES), BF16),
                        pltpu.VMEM((SB_PAIRS, 2 * tq, tq), F32),
                        pltpu.VMEM((SB_PAIRS, tq, LANES), F32),
                        pltpu.VMEM((tq, SB_WIDTH), BF16)],
        compiler_params=_params(("parallel", "arbitrary")),
        name="sb_attn",
    )(q, k, v, h, wo)


def _final_norm_kernel(x_ref, g_ref, out_ref):
    out_ref[0] = _rms(x_ref[...], g_ref[...])


def _final_norm(x, g, *, batch, lp, seq):
    nb = lp // GCHUNK
    return pl.pallas_call(
        _final_norm_kernel,
        grid=(batch, seq // GCHUNK),
        in_specs=[pl.BlockSpec((GCHUNK, D_MODEL), lambda b, r: (b * nb + 1 + r, 0)),
                  pl.BlockSpec((1, D_MODEL), lambda b, r: (0, 0))],
        out_specs=pl.BlockSpec((1, GCHUNK, D_MODEL), lambda b, r: (b, r, 0)),
        out_shape=jax.ShapeDtypeStruct((batch, seq, D_MODEL), F32),
        compiler_params=_params(("parallel", "parallel")),
        name="final_norm",
    )(x, g)


def _row_tile(lp):
    for tm in (640, 512, 384, 256, 128):
        if lp % tm == 0:
            return tm
    raise ValueError(f"padded length {lp} is not a multiple of {GCHUNK}")


def kernel(x, meta_tokens, gdn_norm_g, gdn_w_in, gdn_conv_w, gdn_a_log, gdn_dt_bias, gdn_onorm_g, gdn_w_out,
           kv_norm_g, w_kv, sb_norm_g, sb_w_q, sb_w_o, ffn_norm_g, ffn_w_gate_up, ffn_w_down, final_norm_g):
    batch, seq, _ = x.shape
    assert seq % GCHUNK == 0 and gdn_w_in.shape[0] == 1 and sb_w_q.shape[0] == 1
    lp = GCHUNK + seq
    tm = _row_tile(lp)
    tf = D_FF // 2

    head = jnp.concatenate([jnp.zeros((FRONT, D_MODEL), x.dtype), meta_tokens.astype(x.dtype)], axis=0)
    h = jnp.concatenate([jnp.broadcast_to(head[None], (batch, GCHUNK, D_MODEL)), x], axis=1)
    h = h.reshape(batch * lp, D_MODEL)

    w_in = gdn_w_in[0]
    w_main = w_in[:, :4 * GDN_WIDTH].astype(BF16)
    w_ab = jnp.pad(w_in[:, 4 * GDN_WIDTH:], ((0, 0), (0, LANES - 2 * GDN_HEADS))).astype(BF16)
    ab_par = jnp.zeros((SUBLANES, LANES), F32)
    ab_par = ab_par.at[0, :GDN_HEADS].set(gdn_a_log[0]).at[1, :GDN_HEADS].set(gdn_dt_bias[0])

    q, k, v, gate, gb = _gdn_in(h, gdn_norm_g[0][None], w_main, w_ab, gdn_conv_w[0], ab_par, tm=tm, lp=lp)
    ub, wc, qd, kdt, at, egl = _gdn_prep(q, k, v, gb)
    h = _gdn_scan(ub, wc, qd, kdt, at, egl, gate, h, gdn_onorm_g[0][None], gdn_w_out[0].astype(BF16), batch=batch)
    h = _ffn(h, ffn_norm_g[0][None], ffn_w_gate_up[0].astype(BF16), ffn_w_down[0].astype(BF16), tm=tm, tf=tf)

    sq, sk, sv = _qkv(h, kv_norm_g[None], sb_norm_g[0][None], w_kv.astype(BF16), sb_w_q[0].astype(BF16), tm=tm)
    h = _sb_layer(h, sq, sk, sv, sb_w_o[0].astype(BF16), batch=batch, lp=lp)
    h = _ffn(h, ffn_norm_g[1][None], ffn_w_gate_up[1].astype(BF16), ffn_w_down[1].astype(BF16), tm=tm, tf=tf)
    return _final_norm(h, final_norm_g[None], batch=batch, lp=lp, seq=seq)
```

```python
import functools

import jax
import jax.numpy as jnp
from jax import lax
from jax.experimental import pallas as pl
from jax.experimental.pallas import tpu as pltpu

F32 = jnp.float32
BF16 = jnp.bfloat16

D_MODEL = 1024
N_META = 16
GDN_HEADS = 8
GDN_HEAD_DIM = 128
GDN_WIDTH = GDN_HEADS * GDN_HEAD_DIM
CONV_WIDTH = 4
SB_HEADS = 16
SB_HEAD_DIM = 64
SB_WIDTH = SB_HEADS * SB_HEAD_DIM
D_FF = 2816
EPS = 1e-6

LANES = 128
SUBLANES = 8
GCHUNK = 128
FRONT = GCHUNK - N_META
SB_BLOCK = 128
SB_PAIRS = SB_WIDTH // LANES
SB_SKIP = 100.0
SB_MASKED = -1e30
VMEM_LIMIT = 56 * 1024 * 1024


def _dot(a, b):
    return lax.dot_general(a, b, (((1,), (0,)), ((), ())), preferred_element_type=F32)


def _dot_nt(a, b):
    return lax.dot_general(a, b, (((1,), (1,)), ((), ())), preferred_element_type=F32)


def _split(a):
    hi = a.astype(BF16)
    lo = (a - hi.astype(F32)).astype(BF16)
    return hi, lo


def _dot3(a, b):
    ah, al = _split(a)
    bh, bl = _split(b)
    return _dot(ah, bh) + (_dot(ah, bl) + _dot(al, bh))


def _dot_exact_lhs(m_bf16, b):
    bh, bl = _split(b)
    return _dot(m_bf16, bh) + _dot(m_bf16, bl)


def _dot_exact_rhs(a, m_bf16):
    ah, al = _split(a)
    return _dot(ah, m_bf16) + _dot(al, m_bf16)


def _rms(x, g):
    return x * lax.rsqrt(jnp.mean(x * x, axis=-1, keepdims=True) + EPS) * g


def _sigmoid(x):
    return 1.0 / (1.0 + jnp.exp(-x))


def _softplus(x):
    return jnp.maximum(x, 0.0) + jnp.log(1.0 + jnp.exp(-jnp.abs(x)))


def _params(sem):
    return pltpu.CompilerParams(dimension_semantics=sem, vmem_limit_bytes=VMEM_LIMIT)


def _gdn_in_kernel(h_ref, ng_ref, w_ref, wab_ref, cw_ref, ab_ref,
                   q_ref, k_ref, v_ref, gate_ref, gb_ref,
                   carry_ref, cbuf_ref, *, tm, tiles_per_batch):
    tb = pl.program_id(0) % tiles_per_batch

    @pl.when(tb == 0)
    def _():
        carry_ref[...] = jnp.zeros_like(carry_ref)

    xn = _rms(h_ref[...], ng_ref[...]).astype(BF16)
    outs = (q_ref, k_ref, v_ref)
    for part in range(3):
        cols = slice(part * GDN_WIDTH, (part + 1) * GDN_WIDTH)
        p = _dot(xn, w_ref[:, cols])
        cbuf_ref[0:SUBLANES, :] = carry_ref[part]
        cbuf_ref[SUBLANES:, :] = p
        carry_ref[part] = p[tm - SUBLANES:, :]
        cw = cw_ref[:, cols]
        y = p * cw[CONV_WIDTH - 1:CONV_WIDTH, :]
        for i in range(1, CONV_WIDTH):
            y = y + cbuf_ref[pl.ds(SUBLANES - i, tm), :] * cw[CONV_WIDTH - 1 - i:CONV_WIDTH - i, :]
        y = y * _sigmoid(y)
        if part < 2:
            for hd in range(GDN_HEADS):
                sl = slice(hd * GDN_HEAD_DIM, (hd + 1) * GDN_HEAD_DIM)
                seg = y[:, sl]
                outs[part][:, sl] = seg * lax.rsqrt(jnp.sum(seg * seg, axis=-1, keepdims=True) + EPS)
        else:
            v_ref[...] = y
    gate_ref[...] = _dot(xn, w_ref[:, 3 * GDN_WIDTH:4 * GDN_WIDTH])

    ab = _dot(xn, wab_ref[...])
    a_log = ab_ref[0:1, :]
    dt_bias = ab_ref[1:2, :]
    g = -jnp.exp(a_log) * _softplus(ab + dt_bias)
    beta = _sigmoid(ab)
    lane = lax.broadcasted_iota(jnp.int32, ab.shape, 1)
    pos = tb * tm + lax.broadcasted_iota(jnp.int32, ab.shape, 0)
    gb_ref[...] = jnp.where(pos >= FRONT, jnp.where(lane < GDN_HEADS, g, beta), 0.0)


def _gdn_in(h, ng, w_main, w_ab, conv_w, ab_par, *, tm, lp):
    m = h.shape[0]
    row = lambda r: (r, 0)
    const = lambda r: (0, 0)
    wide = pl.BlockSpec((tm, GDN_WIDTH), row)
    return pl.pallas_call(
        functools.partial(_gdn_in_kernel, tm=tm, tiles_per_batch=lp // tm),
        grid=(m // tm,),
        in_specs=[
            pl.BlockSpec((tm, D_MODEL), row),
            pl.BlockSpec((1, D_MODEL), const),
            pl.BlockSpec((D_MODEL, 4 * GDN_WIDTH), const),
            pl.BlockSpec((D_MODEL, LANES), const),
            pl.BlockSpec((CONV_WIDTH, 3 * GDN_WIDTH), const),
            pl.BlockSpec((SUBLANES, LANES), const),
        ],
        out_specs=[wide, wide, wide, wide, pl.BlockSpec((tm, LANES), row)],
        out_shape=[jax.ShapeDtypeStruct((m, GDN_WIDTH), F32)] * 4 + [jax.ShapeDtypeStruct((m, LANES), F32)],
        scratch_shapes=[
            pltpu.VMEM((3, SUBLANES, GDN_WIDTH), F32),
            pltpu.VMEM((tm + SUBLANES, GDN_WIDTH), F32),
        ],
        compiler_params=_params(("arbitrary",)),
        name="gdn_in",
    )(h, ng, w_main, w_ab, conv_w, ab_par)


def _mm(xs, ys):
    return [_dot(x.astype(BF16), y.astype(BF16)) for x, y in zip(xs, ys)]


def _unit_lower_inverses(ns, blk_same):
    c = ns[0].shape[0]
    eye = (lax.broadcasted_iota(jnp.int32, (c, c), 0) == lax.broadcasted_iota(jnp.int32, (c, c), 1)).astype(F32)
    nd = [jnp.where(blk_same, n, 0.0) for n in ns]
    no = [n - d for n, d in zip(ns, nd)]
    nd2 = _mm(nd, nd)
    nd4 = _mm(nd2, nd2)
    nd8 = _mm(nd4, nd4)
    dinv = _mm([eye - x for x in nd], [eye + x for x in nd2])
    dinv = _mm(dinv, [eye + x for x in nd4])
    dinv = _mm(dinv, [eye + x for x in nd8])
    m1 = _mm(dinv, no)
    m2 = _mm(m1, m1)
    m4 = _mm(m2, m2)
    outer = _mm([eye - x for x in m1], [eye + x for x in m2])
    outer = _mm(outer, [eye + x for x in m4])
    t0 = _mm(outer, dinv)
    res = [(eye - t) - _dot3(n, t) for n, t in zip(ns, t0)]
    return [t + r for t, r in zip(t0, _mm(t0, res))]


def _gdn_prep_kernel(q_ref, k_ref, v_ref, gb_ref,
                     ub_ref, wc_ref, qd_ref, kdt_ref, at_ref, egl_ref):
    c = GCHUNK
    ri = lax.broadcasted_iota(jnp.int32, (c, c), 0)
    ci = lax.broadcasted_iota(jnp.int32, (c, c), 1)
    incl = ri >= ci
    strict = ri > ci
    blk_same = (ri // 16) == (ci // 16)
    tril = incl.astype(BF16)
    scale = GDN_HEAD_DIM ** -0.5
    heads = range(GDN_HEADS)
    sls = [slice(hd * GDN_HEAD_DIM, (hd + 1) * GDN_HEAD_DIM) for hd in heads]
    ks = [k_ref[:, sl] for sl in sls]
    kbs = [k.astype(BF16) for k in ks]
    bmats = [jnp.broadcast_to(gb_ref[:, GDN_HEADS + hd:GDN_HEADS + hd + 1], (c, c)) for hd in heads]
    gcols = [_dot_exact_lhs(tril, jnp.broadcast_to(gb_ref[:, hd:hd + 1], (c, c))) for hd in heads]
    kks = [_dot_nt(kb, kb) for kb in kbs]
    decays = [jnp.where(incl, jnp.exp(jnp.where(incl, g - g.T, 0.0)), 0.0) for g in gcols]
    ns = [jnp.where(strict, b * d * kk, 0.0) for b, d, kk in zip(bmats, decays, kks)]
    ts = _unit_lower_inverses(ns, blk_same)
    egs = [jnp.exp(g) for g in gcols]
    rhs = [jnp.concatenate([b * v_ref[:, sl], (b * eg) * k], axis=1).astype(BF16)
           for b, eg, k, sl in zip(bmats, egs, ks, sls)]
    sols = [_dot(t.astype(BF16), r) for t, r in zip(ts, rhs)]
    qs = [q_ref[:, sl] * scale for sl in sls]
    ats = [_dot_nt(q.astype(BF16), kb) * d for q, kb, d in zip(qs, kbs, decays)]
    for hd in heads:
        sl = sls[hd]
        ub_ref[:, sl] = sols[hd][:, :GDN_HEAD_DIM]
        wc_ref[:, sl] = sols[hd][:, GDN_HEAD_DIM:].astype(BF16)
        at_ref[:, sl] = ats[hd].astype(BF16)
        qd_ref[:, sl] = (qs[hd] * egs[hd]).astype(BF16)
        glast = gcols[hd][c - 1:c, :]
        kdt_ref[:, sl] = (ks[hd] * jnp.exp(glast - gcols[hd])).T.astype(BF16)
        egl_ref[:, sl] = jnp.broadcast_to(jnp.exp(glast), (SUBLANES, GDN_HEAD_DIM))


def _gdn_prep(q, k, v, gb):
    m = q.shape[0]
    nchunks = m // GCHUNK
    row = lambda r: (r, 0)
    wide = pl.BlockSpec((GCHUNK, GDN_WIDTH), row)
    f32w = jax.ShapeDtypeStruct((m, GDN_WIDTH), F32)
    b16w = jax.ShapeDtypeStruct((m, GDN_WIDTH), BF16)
    return pl.pallas_call(
        _gdn_prep_kernel,
        grid=(nchunks,),
        in_specs=[wide, wide, wide, pl.BlockSpec((GCHUNK, LANES), row)],
        out_specs=[wide, wide, wide, wide, wide, pl.BlockSpec((SUBLANES, GDN_WIDTH), row)],
        out_shape=[f32w, b16w, b16w, b16w, b16w,
                   jax.ShapeDtypeStruct((nchunks * SUBLANES, GDN_WIDTH), F32)],
        compiler_params=_params(("parallel",)),
        name="gdn_prep",
    )(q, k, v, gb)


def _gdn_scan_kernel(ub_ref, wc_ref, qd_ref, kdt_ref, at_ref, egl_ref, gate_ref, h_ref, og_ref, wo_ref,
                     out_ref, s_ref, o_ref):
    @pl.when(pl.program_id(1) == 0)
    def _():
        s_ref[...] = jnp.zeros_like(s_ref)

    heads = range(GDN_HEADS)
    sls = [slice(hd * GDN_HEAD_DIM, (hd + 1) * GDN_HEAD_DIM) for hd in heads]
    ss = [s_ref[hd] for hd in heads]
    sbs = [s.astype(BF16) for s in ss]
    ubs = [(ub_ref[:, sl] - _dot(wc_ref[:, sl], sb)).astype(BF16) for sl, sb in zip(sls, sbs)]
    os_ = [_dot(qd_ref[:, sl], sb) + _dot(at_ref[:, sl], ub) for sl, sb, ub in zip(sls, sbs, ubs)]
    for hd in heads:
        s_ref[hd] = ss[hd] * egl_ref[0:1, sls[hd]] + _dot(kdt_ref[:, sls[hd]], ubs[hd])
    for hd in heads:
        o = os_[hd]
        o = o * lax.rsqrt(jnp.mean(o * o, axis=-1, keepdims=True) + EPS) * og_ref[...]
        gate = gate_ref[:, sls[hd]]
        o_ref[:, sls[hd]] = (o * (gate * _sigmoid(gate))).astype(BF16)
    out_ref[...] = h_ref[...] + _dot(o_ref[...], wo_ref[...])


def _gdn_scan(ub, wc, qd, kdt, at, egl, gate, h, og, wo, *, batch):
    m = h.shape[0]
    nch = m // GCHUNK // batch
    row = lambda b, c: (b * nch + c, 0)
    const = lambda b, c: (0, 0)
    wide = pl.BlockSpec((GCHUNK, GDN_WIDTH), row)
    return pl.pallas_call(
        _gdn_scan_kernel,
        grid=(batch, nch),
        in_specs=[wide, wide, wide, wide, wide,
                  pl.BlockSpec((SUBLANES, GDN_WIDTH), row),
                  wide,
                  pl.BlockSpec((GCHUNK, D_MODEL), row),
                  pl.BlockSpec((1, GDN_HEAD_DIM), const),
                  pl.BlockSpec((GDN_WIDTH, D_MODEL), const)],
        out_specs=pl.BlockSpec((GCHUNK, D_MODEL), row),
        out_shape=jax.ShapeDtypeStruct((m, D_MODEL), F32),
        scratch_shapes=[pltpu.VMEM((GDN_HEADS, GDN_HEAD_DIM, GDN_HEAD_DIM), F32),
                        pltpu.VMEM((GCHUNK, GDN_WIDTH), BF16)],
        compiler_params=_params(("arbitrary", "arbitrary")),
        name="gdn_scan",
    )(ub, wc, qd, kdt, at, egl, gate, h, og, wo)


def _ffn_kernel(x_ref, g_ref, wg_ref, wu_ref, wd_ref, out_ref, xn_ref, acc_ref):
    j = pl.program_id(1)

    @pl.when(j == 0)
    def _():
        xn_ref[...] = _rms(x_ref[...], g_ref[...]).astype(BF16)
        acc_ref[...] = x_ref[...]

    xn = xn_ref[...]
    gate = _dot(xn, wg_ref[...])
    up = _dot(xn, wu_ref[...])
    act = (gate * _sigmoid(gate) * up).astype(BF16)
    acc_ref[...] += _dot(act, wd_ref[...])

    @pl.when(j == pl.num_programs(1) - 1)
    def _():
        out_ref[...] = acc_ref[...]


def _ffn(x, g, w_gu, w_d, *, tm, tf):
    m = x.shape[0]
    nf = D_FF // tf
    return pl.pallas_call(
        _ffn_kernel,
        grid=(m // tm, nf),
        in_specs=[
            pl.BlockSpec((tm, D_MODEL), lambda r, j: (r, 0)),
            pl.BlockSpec((1, D_MODEL), lambda r, j: (0, 0)),
            pl.BlockSpec((D_MODEL, tf), lambda r, j: (0, j)),
            pl.BlockSpec((D_MODEL, tf), lambda r, j: (0, nf + j)),
            pl.BlockSpec((tf, D_MODEL), lambda r, j: (j, 0)),
        ],
        out_specs=pl.BlockSpec((tm, D_MODEL), lambda r, j: (r, 0)),
        out_shape=jax.ShapeDtypeStruct((m, D_MODEL), F32),
        scratch_shapes=[pltpu.VMEM((tm, D_MODEL), BF16), pltpu.VMEM((tm, D_MODEL), F32)],
        compiler_params=_params(("parallel", "arbitrary")),
        name="ffn",
    )(x, g, w_gu, w_gu, w_d)


def _qkv_kernel(x_ref, gkv_ref, gq_ref, wkv_ref, wq_ref, q_ref, k_ref, v_ref):
    x = x_ref[...]
    inv = lax.rsqrt(jnp.mean(x * x, axis=-1, keepdims=True) + EPS)
    xkv = (x * inv * gkv_ref[...]).astype(BF16)
    xq = (x * inv * gq_ref[...]).astype(BF16)
    k_ref[...] = _dot(xkv, wkv_ref[:, :SB_WIDTH]).astype(BF16)
    v_ref[...] = _dot(xkv, wkv_ref[:, SB_WIDTH:]).astype(BF16)
    q_ref[...] = (_dot(xq, wq_ref[...]) * (SB_HEAD_DIM ** -0.5)).astype(BF16)


def _qkv(x, gkv, gq, wkv, wq, *, tm):
    m = x.shape[0]
    row = lambda r: (r, 0)
    const = lambda r: (0, 0)
    out = pl.BlockSpec((tm, SB_WIDTH), row)
    return pl.pallas_call(
        _qkv_kernel,
        grid=(m // tm,),
        in_specs=[pl.BlockSpec((tm, D_MODEL), row),
                  pl.BlockSpec((1, D_MODEL), const),
                  pl.BlockSpec((1, D_MODEL), const),
                  pl.BlockSpec((D_MODEL, 2 * SB_WIDTH), const),
                  pl.BlockSpec((D_MODEL, SB_WIDTH), const)],
        out_specs=[out, out, out],
        out_shape=[jax.ShapeDtypeStruct((m, SB_WIDTH), BF16)] * 3,
        compiler_params=_params(("parallel",)),
        name="sb_qkv",
    )(x, gkv, gq, wkv, wq)


def _sb_kernel(q_ref, k_ref, v_ref, h_ref, wo_ref, out_ref, qm_ref, c_ref, o_ref, obf_ref):
    tq = SB_BLOCK
    iq = pl.program_id(1)
    first = lax.broadcasted_iota(jnp.int32, (tq, LANES), 1) < SB_HEAD_DIM
    ri = lax.broadcasted_iota(jnp.int32, (tq, tq), 0)
    ci = lax.broadcasted_iota(jnp.int32, (tq, tq), 1)
    upper_ext = jnp.concatenate([(ri > ci).astype(BF16), jnp.ones((tq, LANES), BF16)], axis=1)

    for p in range(SB_PAIRS):
        q2 = q_ref[:, p * LANES:(p + 1) * LANES]
        zero = jnp.zeros_like(q2)
        qm_ref[p, 0:tq, :] = jnp.where(first, q2, zero)
        qm_ref[p, tq:, :] = jnp.where(first, zero, q2)
    c_ref[...] = jnp.zeros_like(c_ref)
    o_ref[...] = jnp.zeros_like(o_ref)

    def visit(j, bias):
        start = pl.multiple_of(j * tq, tq)
        pairs = range(SB_PAIRS)
        cols = [slice(p * LANES, (p + 1) * LANES) for p in pairs]
        zs = [_dot_nt(qm_ref[p], k_ref[pl.ds(start, tq), cols[p]]) + bias for p in pairs]
        sps = [_softplus(z) for z in zs]
        parts = [_split(sp) for sp in sps]
        accs = [_dot(hi, upper_ext) + _dot(lo, upper_ext) for hi, lo in parts]
        ws = []
        cmin = None
        for p in pairs:
            c = c_ref[p]
            ws.append(jnp.exp(zs[p] - sps[p] - accs[p][:, :tq] - c).astype(BF16))
            c = c + accs[p][:, tq:]
            c_ref[p] = c
            cmin = c if cmin is None else jnp.minimum(cmin, c)
        for p in pairs:
            vj = v_ref[pl.ds(start, tq), cols[p]]
            zero = jnp.zeros_like(vj)
            v2 = jnp.concatenate([jnp.where(first, vj, zero), jnp.where(first, zero, vj)], axis=0)
            o_ref[p] += _dot(jnp.concatenate([ws[p][:tq], ws[p][tq:]], axis=1), v2)
        return jnp.min(cmin)

    qpos = iq * tq + ri
    kpos = iq * tq + ci
    diag = jnp.where(jnp.logical_and(kpos < qpos, kpos >= FRONT), 0.0, SB_MASKED)
    lowest = visit(iq, jnp.concatenate([diag, diag], axis=0))

    def cond(carry):
        j, low = carry
        return jnp.logical_and(j >= 0, low < SB_SKIP)

    def body(carry):
        j, _ = carry
        kp = j * tq + lax.broadcasted_iota(jnp.int32, (1, tq), 1)
        return j - 1, visit(j, jnp.where(kp >= FRONT, 0.0, SB_MASKED))

    lax.while_loop(cond, body, (iq - 1, lowest))

    for p in range(SB_PAIRS):
        obf_ref[:, p * LANES:(p + 1) * LANES] = o_ref[p].astype(BF16)
    out_ref[...] = h_ref[...] + _dot(obf_ref[...], wo_ref[...])


def _sb_layer(h, q, k, v, wo, *, batch, lp):
    m = q.shape[0]
    nq = lp // SB_BLOCK
    tq = SB_BLOCK
    row = lambda b, i: (b * nq + i, 0)
    whole = lambda b, i: (b, 0)
    return pl.pallas_call(
        _sb_kernel,
        grid=(batch, nq),
        in_specs=[pl.BlockSpec((tq, SB_WIDTH), row),
                  pl.BlockSpec((lp, SB_WIDTH), whole, pipeline_mode=pl.Buffered(1)),
                  pl.BlockSpec((lp, SB_WIDTH), whole, pipeline_mode=pl.Buffered(1)),
                  pl.BlockSpec((tq, D_MODEL), row),
                  pl.BlockSpec((SB_WIDTH, D_MODEL), lambda b, i: (0, 0))],
        out_specs=pl.BlockSpec((tq, D_MODEL), row),
        out_shape=jax.ShapeDtypeStruct((m, D_MODEL), F32),
        scratch_shapes=[pltpu.VMEM((SB_PAIRS, 2 * tq, LANES), BF16),
                        pltpu.VMEM((SB_PAIRS, 2 * tq, tq), F32),
                        pltpu.VMEM((SB_PAIRS, tq, LANES), F32),
                        pltpu.VMEM((tq, SB_WIDTH), BF16)],
        compiler_params=_params(("parallel", "arbitrary")),
        name="sb_attn",
    )(q, k, v, h, wo)


def _final_norm_kernel(x_ref, g_ref, out_ref):
    out_ref[0] = _rms(x_ref[...], g_ref[...])


def _final_norm(x, g, *, batch, lp, seq):
    rows = max(r for r in (1024, 512, 256, 128) if seq % r == 0)
    return pl.pallas_call(
        _final_norm_kernel,
        grid=(batch, seq // rows),
        in_specs=[pl.BlockSpec((pl.Element(rows), pl.Element(D_MODEL)),
                               lambda b, r: (pl.multiple_of(b * lp + GCHUNK + r * rows, GCHUNK), 0)),
                  pl.BlockSpec((1, D_MODEL), lambda b, r: (0, 0))],
        out_specs=pl.BlockSpec((1, rows, D_MODEL), lambda b, r: (b, r, 0)),
        out_shape=jax.ShapeDtypeStruct((batch, seq, D_MODEL), F32),
        compiler_params=_params(("parallel", "parallel")),
        name="final_norm",
    )(x, g)


def _row_tile(lp):
    for tm in (640, 512, 384, 256, 128):
        if lp % tm == 0:
            return tm
    raise ValueError(f"padded length {lp} is not a multiple of {GCHUNK}")


def kernel(x, meta_tokens, gdn_norm_g, gdn_w_in, gdn_conv_w, gdn_a_log, gdn_dt_bias, gdn_onorm_g, gdn_w_out,
           kv_norm_g, w_kv, sb_norm_g, sb_w_q, sb_w_o, ffn_norm_g, ffn_w_gate_up, ffn_w_down, final_norm_g):
    batch, seq, _ = x.shape
    assert seq % GCHUNK == 0 and gdn_w_in.shape[0] == 1 and sb_w_q.shape[0] == 1
    lp = GCHUNK + seq
    tm = _row_tile(lp)
    tf = D_FF // 2

    head = jnp.concatenate([jnp.zeros((FRONT, D_MODEL), x.dtype), meta_tokens.astype(x.dtype)], axis=0)
    h = jnp.concatenate([jnp.broadcast_to(head[None], (batch, GCHUNK, D_MODEL)), x], axis=1)
    h = h.reshape(batch * lp, D_MODEL)

    w_in = gdn_w_in[0]
    w_main = w_in[:, :4 * GDN_WIDTH].astype(BF16)
    w_ab = jnp.pad(w_in[:, 4 * GDN_WIDTH:], ((0, 0), (0, LANES - 2 * GDN_HEADS))).astype(BF16)
    ab_par = jnp.zeros((SUBLANES, LANES), F32)
    ab_par = ab_par.at[0, :GDN_HEADS].set(gdn_a_log[0]).at[1, :GDN_HEADS].set(gdn_dt_bias[0])

    q, k, v, gate, gb = _gdn_in(h, gdn_norm_g[0][None], w_main, w_ab, gdn_conv_w[0], ab_par, tm=tm, lp=lp)
    ub, wc, qd, kdt, at, egl = _gdn_prep(q, k, v, gb)
    h = _gdn_scan(ub, wc, qd, kdt, at, egl, gate, h, gdn_onorm_g[0][None], gdn_w_out[0].astype(BF16), batch=batch)
    h = _ffn(h, ffn_norm_g[0][None], ffn_w_gate_up[0].astype(BF16), ffn_w_down[0].astype(BF16), tm=tm, tf=tf)

    sq, sk, sv = _qkv(h, kv_norm_g[None], sb_norm_g[0][None], w_kv.astype(BF16), sb_w_q[0].astype(BF16), tm=tm)
    h = _sb_layer(h, sq, sk, sv, sb_w_o[0].astype(BF16), batch=batch, lp=lp)
    h = _ffn(h, ffn_norm_g[1][None], ffn_w_gate_up[1].astype(BF16), ffn_w_down[1].astype(BF16), tm=tm, tf=tf)
    return _final_norm(h, final_norm_g[None], batch=batch, lp=lp, seq=seq)
```

```python
import functools

import jax
import jax.numpy as jnp
from jax import lax
from jax.experimental import pallas as pl
from jax.experimental.pallas import tpu as pltpu

F32 = jnp.float32
BF16 = jnp.bfloat16

D_MODEL = 1024
N_META = 16
GDN_HEADS = 8
GDN_HEAD_DIM = 128
GDN_WIDTH = GDN_HEADS * GDN_HEAD_DIM
CONV_WIDTH = 4
SB_HEADS = 16
SB_HEAD_DIM = 64
SB_WIDTH = SB_HEADS * SB_HEAD_DIM
D_FF = 2816
EPS = 1e-6

LANES = 128
SUBLANES = 8
GCHUNK = 128
FRONT = GCHUNK - N_META
SB_BLOCK = 128
SB_PAIRS = SB_WIDTH // LANES
SB_SKIP = 100.0
SB_MASKED = -1e30
VMEM_LIMIT = 56 * 1024 * 1024


def _dot(a, b):
    return lax.dot_general(a, b, (((1,), (0,)), ((), ())), preferred_element_type=F32)


def _dot_nt(a, b):
    return lax.dot_general(a, b, (((1,), (1,)), ((), ())), preferred_element_type=F32)


def _split(a):
    hi = a.astype(BF16)
    lo = (a - hi.astype(F32)).astype(BF16)
    return hi, lo


def _dot3(a, b):
    ah, al = _split(a)
    bh, bl = _split(b)
    return _dot(ah, bh) + (_dot(ah, bl) + _dot(al, bh))


def _dot_exact_lhs(m_bf16, b):
    bh, bl = _split(b)
    return _dot(m_bf16, bh) + _dot(m_bf16, bl)


def _rms(x, g):
    return x * lax.rsqrt(jnp.mean(x * x, axis=-1, keepdims=True) + EPS) * g


def _sigmoid(x):
    return 1.0 / (1.0 + jnp.exp(-x))


def _softplus(x):
    return jnp.maximum(x, 0.0) + jnp.log(1.0 + jnp.exp(-jnp.abs(x)))


def _params(sem):
    return pltpu.CompilerParams(dimension_semantics=sem, vmem_limit_bytes=VMEM_LIMIT)


def _gdn_in_kernel(x_ref, head_ref, ng_ref, w_ref, wab_ref, cw_ref, ab_ref,
                   q_ref, k_ref, v_ref, gate_ref, gb_ref,
                   carry_ref, cbuf_ref, *, tm):
    tb = pl.program_id(1)

    @pl.when(tb == 0)
    def _():
        carry_ref[...] = jnp.zeros_like(carry_ref)

    xt = x_ref[0]
    h = jnp.where(tb == 0, jnp.concatenate([head_ref[...], xt[:tm - GCHUNK]], axis=0), xt)
    xn = _rms(h, ng_ref[...]).astype(BF16)
    outs = (q_ref, k_ref, v_ref)
    for part in range(3):
        cols = slice(part * GDN_WIDTH, (part + 1) * GDN_WIDTH)
        p = _dot(xn, w_ref[:, cols])
        cbuf_ref[0:SUBLANES, :] = carry_ref[part]
        cbuf_ref[SUBLANES:, :] = p
        carry_ref[part] = p[tm - SUBLANES:, :]
        cw = cw_ref[:, cols]
        y = p * cw[CONV_WIDTH - 1:CONV_WIDTH, :]
        for i in range(1, CONV_WIDTH):
            y = y + cbuf_ref[pl.ds(SUBLANES - i, tm), :] * cw[CONV_WIDTH - 1 - i:CONV_WIDTH - i, :]
        y = y * _sigmoid(y)
        if part < 2:
            for hd in range(GDN_HEADS):
                sl = slice(hd * GDN_HEAD_DIM, (hd + 1) * GDN_HEAD_DIM)
                seg = y[:, sl]
                outs[part][:, sl] = seg * lax.rsqrt(jnp.sum(seg * seg, axis=-1, keepdims=True) + EPS)
        else:
            v_ref[...] = y
    gate_ref[...] = _dot(xn, w_ref[:, 3 * GDN_WIDTH:4 * GDN_WIDTH])

    ab = _dot(xn, wab_ref[...])
    a_log = ab_ref[0:1, :]
    dt_bias = ab_ref[1:2, :]
    g = -jnp.exp(a_log) * _softplus(ab + dt_bias)
    beta = _sigmoid(ab)
    lane = lax.broadcasted_iota(jnp.int32, ab.shape, 1)
    pos = tb * tm + lax.broadcasted_iota(jnp.int32, ab.shape, 0)
    gb_ref[...] = jnp.where(pos >= FRONT, jnp.where(lane < GDN_HEADS, g, beta), 0.0)


def _gdn_in(x, head, ng, w_main, w_ab, conv_w, ab_par, *, tm, lp):
    batch = x.shape[0]
    m = batch * lp
    tiles = lp // tm
    row = lambda b, t: (b * tiles + t, 0)
    const = lambda b, t: (0, 0)
    wide = pl.BlockSpec((tm, GDN_WIDTH), row)
    return pl.pallas_call(
        functools.partial(_gdn_in_kernel, tm=tm),
        grid=(batch, tiles),
        in_specs=[
            pl.BlockSpec((pl.Element(1), pl.Element(tm), pl.Element(D_MODEL)),
                         lambda b, t: (b, pl.multiple_of(jnp.maximum(t * tm - GCHUNK, 0), GCHUNK), 0)),
            pl.BlockSpec((GCHUNK, D_MODEL), const),
            pl.BlockSpec((1, D_MODEL), const),
            pl.BlockSpec((D_MODEL, 4 * GDN_WIDTH), const),
            pl.BlockSpec((D_MODEL, LANES), const),
            pl.BlockSpec((CONV_WIDTH, 3 * GDN_WIDTH), const),
            pl.BlockSpec((SUBLANES, LANES), const),
        ],
        out_specs=[wide, wide, wide, wide, pl.BlockSpec((tm, LANES), row)],
        out_shape=[jax.ShapeDtypeStruct((m, GDN_WIDTH), F32)] * 4 + [jax.ShapeDtypeStruct((m, LANES), F32)],
        scratch_shapes=[
            pltpu.VMEM((3, SUBLANES, GDN_WIDTH), F32),
            pltpu.VMEM((tm + SUBLANES, GDN_WIDTH), F32),
        ],
        compiler_params=_params(("arbitrary", "arbitrary")),
        name="gdn_in",
    )(x, head, ng, w_main, w_ab, conv_w, ab_par)


def _mm(xs, ys):
    return [_dot(x.astype(BF16), y.astype(BF16)) for x, y in zip(xs, ys)]


def _unit_lower_inverses(ns, blk_same):
    c = ns[0].shape[0]
    eye = (lax.broadcasted_iota(jnp.int32, (c, c), 0) == lax.broadcasted_iota(jnp.int32, (c, c), 1)).astype(F32)
    nd = [jnp.where(blk_same, n, 0.0) for n in ns]
    no = [n - d for n, d in zip(ns, nd)]
    nd2 = _mm(nd, nd)
    nd4 = _mm(nd2, nd2)
    nd8 = _mm(nd4, nd4)
    dinv = _mm([eye - x for x in nd], [eye + x for x in nd2])
    dinv = _mm(dinv, [eye + x for x in nd4])
    dinv = _mm(dinv, [eye + x for x in nd8])
    m1 = _mm(dinv, no)
    m2 = _mm(m1, m1)
    m4 = _mm(m2, m2)
    outer = _mm([eye - x for x in m1], [eye + x for x in m2])
    outer = _mm(outer, [eye + x for x in m4])
    t0 = _mm(outer, dinv)
    res = [(eye - t) - _dot3(n, t) for n, t in zip(ns, t0)]
    return [t + r for t, r in zip(t0, _mm(t0, res))]


def _gdn_core_kernel(q_ref, k_ref, v_ref, gb_ref, gate_ref, x_ref, head_ref, og_ref, wo_ref,
                     out_ref, s_ref, o_ref):
    chunk = pl.program_id(1)

    @pl.when(chunk == 0)
    def _():
        s_ref[...] = jnp.zeros_like(s_ref)

    c = GCHUNK
    ri = lax.broadcasted_iota(jnp.int32, (c, c), 0)
    ci = lax.broadcasted_iota(jnp.int32, (c, c), 1)
    incl = ri >= ci
    strict = ri > ci
    blk_same = (ri // 16) == (ci // 16)
    tril = incl.astype(BF16)
    scale = GDN_HEAD_DIM ** -0.5
    heads = range(GDN_HEADS)
    sls = [slice(hd * GDN_HEAD_DIM, (hd + 1) * GDN_HEAD_DIM) for hd in heads]
    ks = [k_ref[:, sl] for sl in sls]
    kbs = [k.astype(BF16) for k in ks]
    bmats = [jnp.broadcast_to(gb_ref[:, GDN_HEADS + hd:GDN_HEADS + hd + 1], (c, c)) for hd in heads]
    gcols = [_dot_exact_lhs(tril, jnp.broadcast_to(gb_ref[:, hd:hd + 1], (c, c))) for hd in heads]
    kks = [_dot_nt(kb, kb) for kb in kbs]
    decays = [jnp.where(incl, jnp.exp(jnp.where(incl, g - g.T, 0.0)), 0.0) for g in gcols]
    ns = [jnp.where(strict, b * d * kk, 0.0) for b, d, kk in zip(bmats, decays, kks)]
    ts = _unit_lower_inverses(ns, blk_same)
    egs = [jnp.exp(g) for g in gcols]
    rhs = [jnp.concatenate([b * v_ref[:, sl], (b * eg) * k], axis=1).astype(BF16)
           for b, eg, k, sl in zip(bmats, egs, ks, sls)]
    sols = [_dot(t.astype(BF16), r) for t, r in zip(ts, rhs)]
    qs = [q_ref[:, sl] * scale for sl in sls]
    ats = [_dot_nt(q.astype(BF16), kb) * d for q, kb, d in zip(qs, kbs, decays)]
    glasts = [g[c - 1:c, :] for g in gcols]
    wcs = [s[:, GDN_HEAD_DIM:].astype(BF16) for s in sols]
    qds = [(q * eg).astype(BF16) for q, eg in zip(qs, egs)]
    kdts = [(k * jnp.exp(gl - g)).T.astype(BF16) for k, gl, g in zip(ks, glasts, gcols)]
    atbs = [a.astype(BF16) for a in ats]

    ss = [s_ref[hd] for hd in heads]
    sbs = [s.astype(BF16) for s in ss]
    ubs = [(sol[:, :GDN_HEAD_DIM] - _dot(wc, sb)).astype(BF16) for sol, wc, sb in zip(sols, wcs, sbs)]
    os_ = [_dot(qd, sb) + _dot(at, ub) for qd, sb, at, ub in zip(qds, sbs, atbs, ubs)]
    for hd in heads:
        s_ref[hd] = ss[hd] * jnp.exp(glasts[hd]) + _dot(kdts[hd], ubs[hd])
    for hd in heads:
        o = os_[hd]
        o = o * lax.rsqrt(jnp.mean(o * o, axis=-1, keepdims=True) + EPS) * og_ref[...]
        gate = gate_ref[:, sls[hd]]
        o_ref[:, sls[hd]] = (o * (gate * _sigmoid(gate))).astype(BF16)
    resid = jnp.where(chunk == 0, head_ref[...], x_ref[0])
    out_ref[...] = resid + _dot(o_ref[...], wo_ref[...])


def _gdn_core(q, k, v, gb, gate, x, head, og, wo):
    m = q.shape[0]
    batch = x.shape[0]
    nch = m // GCHUNK // batch
    row = lambda b, c: (b * nch + c, 0)
    const = lambda b, c: (0, 0)
    wide = pl.BlockSpec((GCHUNK, GDN_WIDTH), row)
    return pl.pallas_call(
        _gdn_core_kernel,
        grid=(batch, nch),
        in_specs=[wide, wide, wide,
                  pl.BlockSpec((GCHUNK, LANES), row),
                  wide,
                  pl.BlockSpec((1, GCHUNK, D_MODEL), lambda b, c: (b, jnp.maximum(c - 1, 0), 0)),
                  pl.BlockSpec((GCHUNK, D_MODEL), const),
                  pl.BlockSpec((1, GDN_HEAD_DIM), const),
                  pl.BlockSpec((GDN_WIDTH, D_MODEL), const)],
        out_specs=pl.BlockSpec((GCHUNK, D_MODEL), row),
        out_shape=jax.ShapeDtypeStruct((m, D_MODEL), F32),
        scratch_shapes=[pltpu.VMEM((GDN_HEADS, GDN_HEAD_DIM, GDN_HEAD_DIM), F32),
                        pltpu.VMEM((GCHUNK, GDN_WIDTH), BF16)],
        compiler_params=_params(("arbitrary", "arbitrary")),
        name="gdn_core",
    )(q, k, v, gb, gate, x, head, og, wo)


def _ffn_kernel(x_ref, g_ref, wg_ref, wu_ref, wd_ref, out_ref, xn_ref, acc_ref):
    j = pl.program_id(1)

    @pl.when(j == 0)
    def _():
        xn_ref[...] = _rms(x_ref[...], g_ref[...]).astype(BF16)
        acc_ref[...] = x_ref[...]

    xn = xn_ref[...]
    gate = _dot(xn, wg_ref[...])
    up = _dot(xn, wu_ref[...])
    act = (gate * _sigmoid(gate) * up).astype(BF16)
    acc_ref[...] += _dot(act, wd_ref[...])

    @pl.when(j == pl.num_programs(1) - 1)
    def _():
        out_ref[...] = acc_ref[...]


def _ffn(x, g, w_gu, w_d, *, tm, tf):
    m = x.shape[0]
    nf = D_FF // tf
    return pl.pallas_call(
        _ffn_kernel,
        grid=(m // tm, nf),
        in_specs=[
            pl.BlockSpec((tm, D_MODEL), lambda r, j: (r, 0)),
            pl.BlockSpec((1, D_MODEL), lambda r, j: (0, 0)),
            pl.BlockSpec((D_MODEL, tf), lambda r, j: (0, j)),
            pl.BlockSpec((D_MODEL, tf), lambda r, j: (0, nf + j)),
            pl.BlockSpec((tf, D_MODEL), lambda r, j: (j, 0)),
        ],
        out_specs=pl.BlockSpec((tm, D_MODEL), lambda r, j: (r, 0)),
        out_shape=jax.ShapeDtypeStruct((m, D_MODEL), F32),
        scratch_shapes=[pltpu.VMEM((tm, D_MODEL), BF16), pltpu.VMEM((tm, D_MODEL), F32)],
        compiler_params=_params(("parallel", "arbitrary")),
        name="ffn",
    )(x, g, w_gu, w_gu, w_d)


def _qkv_kernel(x_ref, gkv_ref, gq_ref, wkv_ref, wq_ref, q_ref, k_ref, v_ref):
    x = x_ref[...]
    inv = lax.rsqrt(jnp.mean(x * x, axis=-1, keepdims=True) + EPS)
    xkv = (x * inv * gkv_ref[...]).astype(BF16)
    xq = (x * inv * gq_ref[...]).astype(BF16)
    k_ref[...] = _dot(xkv, wkv_ref[:, :SB_WIDTH]).astype(BF16)
    v_ref[...] = _dot(xkv, wkv_ref[:, SB_WIDTH:]).astype(BF16)
    q_ref[...] = (_dot(xq, wq_ref[...]) * (SB_HEAD_DIM ** -0.5)).astype(BF16)


def _qkv(x, gkv, gq, wkv, wq, *, tm):
    m = x.shape[0]
    row = lambda r: (r, 0)
    const = lambda r: (0, 0)
    out = pl.BlockSpec((tm, SB_WIDTH), row)
    return pl.pallas_call(
        _qkv_kernel,
        grid=(m // tm,),
        in_specs=[pl.BlockSpec((tm, D_MODEL), row),
                  pl.BlockSpec((1, D_MODEL), const),
                  pl.BlockSpec((1, D_MODEL), const),
                  pl.BlockSpec((D_MODEL, 2 * SB_WIDTH), const),
                  pl.BlockSpec((D_MODEL, SB_WIDTH), const)],
        out_specs=[out, out, out],
        out_shape=[jax.ShapeDtypeStruct((m, SB_WIDTH), BF16)] * 3,
        compiler_params=_params(("parallel",)),
        name="sb_qkv",
    )(x, gkv, gq, wkv, wq)


def _sb_kernel(q_ref, k_ref, v_ref, h_ref, wo_ref, out_ref, qm_ref, c_ref, o_ref, obf_ref):
    tq = SB_BLOCK
    iq = pl.program_id(1)
    first = lax.broadcasted_iota(jnp.int32, (tq, LANES), 1) < SB_HEAD_DIM
    ri = lax.broadcasted_iota(jnp.int32, (tq, tq), 0)
    ci = lax.broadcasted_iota(jnp.int32, (tq, tq), 1)
    upper_ext = jnp.concatenate([(ri > ci).astype(BF16), jnp.ones((tq, LANES), BF16)], axis=1)

    for p in range(SB_PAIRS):
        q2 = q_ref[:, p * LANES:(p + 1) * LANES]
        zero = jnp.zeros_like(q2)
        qm_ref[p, 0:tq, :] = jnp.where(first, q2, zero)
        qm_ref[p, tq:, :] = jnp.where(first, zero, q2)
    c_ref[...] = jnp.zeros_like(c_ref)
    o_ref[...] = jnp.zeros_like(o_ref)

    def visit(j, bias):
        start = pl.multiple_of(j * tq, tq)
        pairs = range(SB_PAIRS)
        cols = [slice(p * LANES, (p + 1) * LANES) for p in pairs]
        zs = [_dot_nt(qm_ref[p], k_ref[pl.ds(start, tq), cols[p]]) + bias for p in pairs]
        sps = [_softplus(z) for z in zs]
        accs = [_dot(sp.astype(BF16), upper_ext) for sp in sps]
        ws = []
        cmin = None
        for p in pairs:
            c = c_ref[p]
            ws.append(jnp.exp(zs[p] - sps[p] - accs[p][:, :tq] - c).astype(BF16))
            c = c + accs[p][:, tq:]
            c_ref[p] = c
            cmin = c if cmin is None else jnp.minimum(cmin, c)
        for p in pairs:
            vj = v_ref[pl.ds(start, tq), cols[p]]
            zero = jnp.zeros_like(vj)
            v2 = jnp.concatenate([jnp.where(first, vj, zero), jnp.where(first, zero, vj)], axis=0)
            o_ref[p] += _dot(jnp.concatenate([ws[p][:tq], ws[p][tq:]], axis=1), v2)
        return jnp.min(cmin)

    qpos = iq * tq + ri
    kpos = iq * tq + ci
    diag = jnp.where(jnp.logical_and(kpos < qpos, kpos >= FRONT), 0.0, SB_MASKED)
    lowest = visit(iq, jnp.concatenate([diag, diag], axis=0))

    def cond(carry):
        j, low = carry
        return jnp.logical_and(j >= 0, low < SB_SKIP)

    def body(carry):
        j, _ = carry
        kp = j * tq + lax.broadcasted_iota(jnp.int32, (1, tq), 1)
        return j - 1, visit(j, jnp.where(kp >= FRONT, 0.0, SB_MASKED))

    lax.while_loop(cond, body, (iq - 1, lowest))

    for p in range(SB_PAIRS):
        obf_ref[:, p * LANES:(p + 1) * LANES] = o_ref[p].astype(BF16)
    out_ref[...] = h_ref[...] + _dot(obf_ref[...], wo_ref[...])


def _sb_layer(h, q, k, v, wo, *, batch, lp):
    m = q.shape[0]
    nq = lp // SB_BLOCK
    tq = SB_BLOCK
    row = lambda b, i: (b * nq + i, 0)
    whole = lambda b, i: (b, 0)
    return pl.pallas_call(
        _sb_kernel,
        grid=(batch, nq),
        in_specs=[pl.BlockSpec((tq, SB_WIDTH), row),
                  pl.BlockSpec((lp, SB_WIDTH), whole, pipeline_mode=pl.Buffered(1)),
                  pl.BlockSpec((lp, SB_WIDTH), whole, pipeline_mode=pl.Buffered(1)),
                  pl.BlockSpec((tq, D_MODEL), row),
                  pl.BlockSpec((SB_WIDTH, D_MODEL), lambda b, i: (0, 0))],
        out_specs=pl.BlockSpec((tq, D_MODEL), row),
        out_shape=jax.ShapeDtypeStruct((m, D_MODEL), F32),
        scratch_shapes=[pltpu.VMEM((SB_PAIRS, 2 * tq, LANES), BF16),
                        pltpu.VMEM((SB_PAIRS, 2 * tq, tq), F32),
                        pltpu.VMEM((SB_PAIRS, tq, LANES), F32),
                        pltpu.VMEM((tq, SB_WIDTH), BF16)],
        compiler_params=_params(("parallel", "arbitrary")),
        name="sb_attn",
    )(q, k, v, h, wo)


def _final_norm_kernel(x_ref, g_ref, out_ref):
    out_ref[0] = _rms(x_ref[...], g_ref[...])


def _final_norm(x, g, *, batch, lp, seq):
    rows = max(r for r in (1024, 512, 256, 128) if seq % r == 0)
    return pl.pallas_call(
        _final_norm_kernel,
        grid=(batch, seq // rows),
        in_specs=[pl.BlockSpec((pl.Element(rows), pl.Element(D_MODEL)),
                               lambda b, r: (pl.multiple_of(b * lp + GCHUNK + r * rows, GCHUNK), 0)),
                  pl.BlockSpec((1, D_MODEL), lambda b, r: (0, 0))],
        out_specs=pl.BlockSpec((1, rows, D_MODEL), lambda b, r: (b, r, 0)),
        out_shape=jax.ShapeDtypeStruct((batch, seq, D_MODEL), F32),
        compiler_params=_params(("parallel", "parallel")),
        name="final_norm",
    )(x, g)


def _row_tile(lp):
    for tm in (640, 512, 384, 256, 128):
        if lp % tm == 0:
            return tm
    raise ValueError(f"padded length {lp} is not a multiple of {GCHUNK}")


def kernel(x, meta_tokens, gdn_norm_g, gdn_w_in, gdn_conv_w, gdn_a_log, gdn_dt_bias, gdn_onorm_g, gdn_w_out,
           kv_norm_g, w_kv, sb_norm_g, sb_w_q, sb_w_o, ffn_norm_g, ffn_w_gate_up, ffn_w_down, final_norm_g):
    batch, seq, _ = x.shape
    assert seq % GCHUNK == 0 and gdn_w_in.shape[0] == 1 and sb_w_q.shape[0] == 1
    lp = GCHUNK + seq
    tm = _row_tile(lp)
    tf = D_FF // 2

    assert tm > GCHUNK and seq >= tm
    head = jnp.concatenate([jnp.zeros((FRONT, D_MODEL), x.dtype), meta_tokens.astype(x.dtype)], axis=0)

    w_in = gdn_w_in[0]
    w_main = w_in[:, :4 * GDN_WIDTH].astype(BF16)
    w_ab = jnp.pad(w_in[:, 4 * GDN_WIDTH:], ((0, 0), (0, LANES - 2 * GDN_HEADS))).astype(BF16)
    ab_par = jnp.zeros((SUBLANES, LANES), F32)
    ab_par = ab_par.at[0, :GDN_HEADS].set(gdn_a_log[0]).at[1, :GDN_HEADS].set(gdn_dt_bias[0])

    q, k, v, gate, gb = _gdn_in(x, head, gdn_norm_g[0][None], w_main, w_ab, gdn_conv_w[0], ab_par, tm=tm, lp=lp)
    h = _gdn_core(q, k, v, gb, gate, x, head, gdn_onorm_g[0][None], gdn_w_out[0].astype(BF16))
    h = _ffn(h, ffn_norm_g[0][None], ffn_w_gate_up[0].astype(BF16), ffn_w_down[0].astype(BF16), tm=tm, tf=tf)

    sq, sk, sv = _qkv(h, kv_norm_g[None], sb_norm_g[0][None], w_kv.astype(BF16), sb_w_q[0].astype(BF16), tm=tm)
    h = _sb_layer(h, sq, sk, sv, sb_w_o[0].astype(BF16), batch=batch, lp=lp)
    h = _ffn(h, ffn_norm_g[1][None], ffn_w_gate_up[1].astype(BF16), ffn_w_down[1].astype(BF16), tm=tm, tf=tf)
    return _final_norm(h, final_norm_g[None], batch=batch, lp=lp, seq=seq)
```

```python
import functools

import jax
import jax.numpy as jnp
from jax import lax
from jax.experimental import pallas as pl
from jax.experimental.pallas import tpu as pltpu

F32 = jnp.float32
BF16 = jnp.bfloat16

D_MODEL = 1024
N_META = 16
GDN_HEADS = 8
GDN_HEAD_DIM = 128
GDN_WIDTH = GDN_HEADS * GDN_HEAD_DIM
CONV_WIDTH = 4
SB_HEADS = 16
SB_HEAD_DIM = 64
SB_WIDTH = SB_HEADS * SB_HEAD_DIM
D_FF = 2816
EPS = 1e-6

LANES = 128
SUBLANES = 8
MXU_DIM = 256
GCHUNK = 128
FRONT = GCHUNK - N_META
SB_BLOCK = 128
SB_PAIRS = SB_WIDTH // LANES
SB_SKIP = 100.0
SB_MASKED = -1e30
VMEM_LIMIT = 56 * 1024 * 1024


def _dot(a, b):
    return lax.dot_general(a, b, (((1,), (0,)), ((), ())), preferred_element_type=F32)


def _dot_nt(a, b):
    return lax.dot_general(a, b, (((1,), (1,)), ((), ())), preferred_element_type=F32)


def _split(a):
    hi = a.astype(BF16)
    lo = (a - hi.astype(F32)).astype(BF16)
    return hi, lo


def _dot3(a, b):
    ah, al = _split(a)
    bh, bl = _split(b)
    return _dot(ah, bh) + (_dot(ah, bl) + _dot(al, bh))


def _dot_exact_lhs(m_bf16, b):
    bh, bl = _split(b)
    return _dot(m_bf16, bh) + _dot(m_bf16, bl)


def _rms(x, g):
    return x * lax.rsqrt(jnp.mean(x * x, axis=-1, keepdims=True) + EPS) * g


def _sigmoid(x):
    return 1.0 / (1.0 + jnp.exp(-x))


def _softplus(x):
    return jnp.maximum(x, 0.0) + jnp.log(1.0 + jnp.exp(-jnp.abs(x)))


def _params(sem):
    return pltpu.CompilerParams(dimension_semantics=sem, vmem_limit_bytes=VMEM_LIMIT)


def _gdn_in_kernel(x_ref, head_ref, ng_ref, w_ref, wab_ref, cw_ref, ab_ref,
                   q_ref, k_ref, v_ref, gate_ref, gb_ref,
                   carry_ref, *, tm):
    tb = pl.program_id(1)

    @pl.when(tb == 0)
    def _():
        carry_ref[...] = jnp.zeros_like(carry_ref)

    xt = x_ref[0]
    h = jnp.where(tb == 0, jnp.concatenate([head_ref[...], xt[:tm - GCHUNK]], axis=0), xt)
    xn = _rms(h, ng_ref[...]).astype(BF16)
    outs = (q_ref, k_ref, v_ref)

    def project(part):
        return _dot(xn, w_ref[:, part * GDN_WIDTH:(part + 1) * GDN_WIDTH])

    pending = project(0)
    for part in range(3):
        cols = slice(part * GDN_WIDTH, (part + 1) * GDN_WIDTH)
        p = pending
        pending = project(part + 1)
        ext = jnp.concatenate([carry_ref[part], p], axis=0)
        carry_ref[part] = p[tm - SUBLANES:, :]
        y = p * cw_ref[CONV_WIDTH - 1:CONV_WIDTH, cols]
        for i in range(1, CONV_WIDTH):
            y = y + pltpu.roll(ext, i, axis=0)[SUBLANES:] * cw_ref[CONV_WIDTH - 1 - i:CONV_WIDTH - i, cols]
        y = y * _sigmoid(y)
        if part < 2:
            for hd in range(GDN_HEADS):
                sl = slice(hd * GDN_HEAD_DIM, (hd + 1) * GDN_HEAD_DIM)
                seg = y[:, sl]
                outs[part][:, sl] = seg * lax.rsqrt(jnp.sum(seg * seg, axis=-1, keepdims=True) + EPS)
        else:
            v_ref[...] = y
    gate_ref[...] = pending

    ab = _dot(xn, wab_ref[...])
    a_log = ab_ref[0:1, :]
    dt_bias = ab_ref[1:2, :]
    g = -jnp.exp(a_log) * _softplus(ab + dt_bias)
    beta = _sigmoid(ab)
    lane = lax.broadcasted_iota(jnp.int32, ab.shape, 1)
    pos = tb * tm + lax.broadcasted_iota(jnp.int32, ab.shape, 0)
    gb_ref[...] = jnp.where(pos >= FRONT, jnp.where(lane < GDN_HEADS, g, beta), 0.0)


def _gdn_in(x, head, ng, w_main, w_ab, conv_w, ab_par, *, tm, lp):
    batch = x.shape[0]
    m = batch * lp
    tiles = lp // tm
    row = lambda b, t: (b * tiles + t, 0)
    const = lambda b, t: (0, 0)
    wide = pl.BlockSpec((tm, GDN_WIDTH), row)
    return pl.pallas_call(
        functools.partial(_gdn_in_kernel, tm=tm),
        grid=(batch, tiles),
        in_specs=[
            pl.BlockSpec((pl.Element(1), pl.Element(tm), pl.Element(D_MODEL)),
                         lambda b, t: (b, pl.multiple_of(jnp.maximum(t * tm - GCHUNK, 0), GCHUNK), 0)),
            pl.BlockSpec((GCHUNK, D_MODEL), const),
            pl.BlockSpec((1, D_MODEL), const),
            pl.BlockSpec((D_MODEL, 4 * GDN_WIDTH), const),
            pl.BlockSpec((D_MODEL, LANES), const),
            pl.BlockSpec((CONV_WIDTH, 3 * GDN_WIDTH), const),
            pl.BlockSpec((SUBLANES, LANES), const),
        ],
        out_specs=[wide, wide, wide, wide, pl.BlockSpec((tm, LANES), row)],
        out_shape=[jax.ShapeDtypeStruct((m, GDN_WIDTH), F32)] * 4 + [jax.ShapeDtypeStruct((m, LANES), F32)],
        scratch_shapes=[pltpu.VMEM((3, SUBLANES, GDN_WIDTH), F32)],
        compiler_params=_params(("arbitrary", "arbitrary")),
        name="gdn_in",
    )(x, head, ng, w_main, w_ab, conv_w, ab_par)


def _mm(xs, ys):
    return [_dot(x.astype(BF16), y.astype(BF16)) for x, y in zip(xs, ys)]


def _unit_lower_inverses(ns, blk_same):
    c = ns[0].shape[0]
    eye = (lax.broadcasted_iota(jnp.int32, (c, c), 0) == lax.broadcasted_iota(jnp.int32, (c, c), 1)).astype(F32)
    nd = [jnp.where(blk_same, n, 0.0) for n in ns]
    no = [n - d for n, d in zip(ns, nd)]
    nd2 = _mm(nd, nd)
    nd4 = _mm(nd2, nd2)
    nd8 = _mm(nd4, nd4)
    dinv = _mm([eye - x for x in nd], [eye + x for x in nd2])
    dinv = _mm(dinv, [eye + x for x in nd4])
    dinv = _mm(dinv, [eye + x for x in nd8])
    m1 = _mm(dinv, no)
    m2 = _mm(m1, m1)
    m4 = _mm(m2, m2)
    outer = _mm([eye - x for x in m1], [eye + x for x in m2])
    outer = _mm(outer, [eye + x for x in m4])
    t0 = _mm(outer, dinv)
    res = [(eye - t) - _dot3(n, t) for n, t in zip(ns, t0)]
    return [t + r for t, r in zip(t0, _mm(t0, res))]


def _gdn_core_kernel(q_ref, k_ref, v_ref, gb_ref, gate_ref, x_ref, head_ref, og_ref, wo_ref,
                     out_ref, s_ref, o_ref):
    chunk = pl.program_id(1)

    @pl.when(chunk == 0)
    def _():
        s_ref[...] = jnp.zeros_like(s_ref)

    c = GCHUNK
    ri = lax.broadcasted_iota(jnp.int32, (c, c), 0)
    ci = lax.broadcasted_iota(jnp.int32, (c, c), 1)
    incl = ri >= ci
    strict = ri > ci
    blk_same = (ri // 16) == (ci // 16)
    tril = incl.astype(BF16)
    scale = GDN_HEAD_DIM ** -0.5
    heads = range(GDN_HEADS)
    sls = [slice(hd * GDN_HEAD_DIM, (hd + 1) * GDN_HEAD_DIM) for hd in heads]
    ks = [k_ref[:, sl] for sl in sls]
    kbs = [k.astype(BF16) for k in ks]
    bmats = [jnp.broadcast_to(gb_ref[:, GDN_HEADS + hd:GDN_HEADS + hd + 1], (c, c)) for hd in heads]
    gcols = [_dot_exact_lhs(tril, jnp.broadcast_to(gb_ref[:, hd:hd + 1], (c, c))) for hd in heads]
    kks = [_dot_nt(kb, kb) for kb in kbs]
    decays = [jnp.where(incl, jnp.exp(jnp.where(incl, g - g.T, 0.0)), 0.0) for g in gcols]
    ns = [jnp.where(strict, b * d * kk, 0.0) for b, d, kk in zip(bmats, decays, kks)]
    ts = _unit_lower_inverses(ns, blk_same)
    egs = [jnp.exp(g) for g in gcols]
    rhs = [jnp.concatenate([b * v_ref[:, sl], (b * eg) * k], axis=1).astype(BF16)
           for b, eg, k, sl in zip(bmats, egs, ks, sls)]
    sols = [_dot(t.astype(BF16), r) for t, r in zip(ts, rhs)]
    qs = [q_ref[:, sl] * scale for sl in sls]
    ats = [_dot_nt(q.astype(BF16), kb) * d for q, kb, d in zip(qs, kbs, decays)]
    glasts = [g[c - 1:c, :] for g in gcols]
    wcs = [s[:, GDN_HEAD_DIM:].astype(BF16) for s in sols]
    qds = [(q * eg).astype(BF16) for q, eg in zip(qs, egs)]
    kdts = [(k * jnp.exp(gl - g)).T.astype(BF16) for k, gl, g in zip(ks, glasts, gcols)]
    atbs = [a.astype(BF16) for a in ats]

    ss = [s_ref[hd] for hd in heads]
    sbs = [s.astype(BF16) for s in ss]
    ubs = [(sol[:, :GDN_HEAD_DIM] - _dot(wc, sb)).astype(BF16) for sol, wc, sb in zip(sols, wcs, sbs)]
    os_ = [_dot(qd, sb) + _dot(at, ub) for qd, sb, at, ub in zip(qds, sbs, atbs, ubs)]
    for hd in heads:
        s_ref[hd] = ss[hd] * jnp.exp(glasts[hd]) + _dot(kdts[hd], ubs[hd])
    for hd in heads:
        o = os_[hd]
        o = o * lax.rsqrt(jnp.mean(o * o, axis=-1, keepdims=True) + EPS) * og_ref[...]
        gate = gate_ref[:, sls[hd]]
        o_ref[:, sls[hd]] = (o * (gate * _sigmoid(gate))).astype(BF16)
    resid = jnp.where(chunk == 0, head_ref[...], x_ref[0])
    out_ref[...] = resid + _dot(o_ref[...], wo_ref[...])


def _gdn_core(q, k, v, gb, gate, x, head, og, wo):
    m = q.shape[0]
    batch = x.shape[0]
    nch = m // GCHUNK // batch
    row = lambda b, c: (b * nch + c, 0)
    const = lambda b, c: (0, 0)
    wide = pl.BlockSpec((GCHUNK, GDN_WIDTH), row)
    return pl.pallas_call(
        _gdn_core_kernel,
        grid=(batch, nch),
        in_specs=[wide, wide, wide,
                  pl.BlockSpec((GCHUNK, LANES), row),
                  wide,
                  pl.BlockSpec((1, GCHUNK, D_MODEL), lambda b, c: (b, jnp.maximum(c - 1, 0), 0)),
                  pl.BlockSpec((GCHUNK, D_MODEL), const),
                  pl.BlockSpec((1, GDN_HEAD_DIM), const),
                  pl.BlockSpec((GDN_WIDTH, D_MODEL), const)],
        out_specs=pl.BlockSpec((GCHUNK, D_MODEL), row),
        out_shape=jax.ShapeDtypeStruct((m, D_MODEL), F32),
        scratch_shapes=[pltpu.VMEM((GDN_HEADS, GDN_HEAD_DIM, GDN_HEAD_DIM), F32),
                        pltpu.VMEM((GCHUNK, GDN_WIDTH), BF16)],
        compiler_params=_params(("arbitrary", "arbitrary")),
        name="gdn_core",
    )(q, k, v, gb, gate, x, head, og, wo)


def _ffn_chunks():
    step = 3 * MXU_DIM
    return [(lo, min(lo + step, D_FF)) for lo in range(0, D_FF, step)]


def _ffn_kernel(x_ref, g_ref, wgu_ref, wd_ref, out_ref):
    x = x_ref[...]
    xn = _rms(x, g_ref[...]).astype(BF16)

    def gate_up(lo, hi):
        return _dot(xn, wgu_ref[:, lo:hi]), _dot(xn, wgu_ref[:, D_FF + lo:D_FF + hi])

    chunks = _ffn_chunks()
    acc = x
    pending = gate_up(*chunks[0])
    for i, (lo, hi) in enumerate(chunks):
        gate, up = pending
        if i + 1 < len(chunks):
            pending = gate_up(*chunks[i + 1])
        act = (gate * _sigmoid(gate) * up).astype(BF16)
        acc = acc + _dot(act, wd_ref[lo:hi, :])
    out_ref[...] = acc


def _ffn(x, g, w_gu, w_d, *, tm):
    m = x.shape[0]
    row = lambda r: (r, 0)
    const = lambda r: (0, 0)
    return pl.pallas_call(
        _ffn_kernel,
        grid=(m // tm,),
        in_specs=[
            pl.BlockSpec((tm, D_MODEL), row),
            pl.BlockSpec((1, D_MODEL), const),
            pl.BlockSpec((D_MODEL, 2 * D_FF), const, pipeline_mode=pl.Buffered(1)),
            pl.BlockSpec((D_FF, D_MODEL), const, pipeline_mode=pl.Buffered(1)),
        ],
        out_specs=pl.BlockSpec((tm, D_MODEL), row),
        out_shape=jax.ShapeDtypeStruct((m, D_MODEL), F32),
        compiler_params=_params(("parallel",)),
        name="ffn",
    )(x, g, w_gu, w_d)


def _qkv_kernel(x_ref, gkv_ref, gq_ref, wkv_ref, wq_ref, q_ref, k_ref, v_ref):
    x = x_ref[...]
    inv = lax.rsqrt(jnp.mean(x * x, axis=-1, keepdims=True) + EPS)
    xkv = (x * inv * gkv_ref[...]).astype(BF16)
    xq = (x * inv * gq_ref[...]).astype(BF16)
    k_ref[...] = _dot(xkv, wkv_ref[:, :SB_WIDTH]).astype(BF16)
    v_ref[...] = _dot(xkv, wkv_ref[:, SB_WIDTH:]).astype(BF16)
    q_ref[...] = (_dot(xq, wq_ref[...]) * (SB_HEAD_DIM ** -0.5)).astype(BF16)


def _qkv(x, gkv, gq, wkv, wq, *, tm):
    m = x.shape[0]
    row = lambda r: (r, 0)
    const = lambda r: (0, 0)
    out = pl.BlockSpec((tm, SB_WIDTH), row)
    return pl.pallas_call(
        _qkv_kernel,
        grid=(m // tm,),
        in_specs=[pl.BlockSpec((tm, D_MODEL), row),
                  pl.BlockSpec((1, D_MODEL), const),
                  pl.BlockSpec((1, D_MODEL), const),
                  pl.BlockSpec((D_MODEL, 2 * SB_WIDTH), const),
                  pl.BlockSpec((D_MODEL, SB_WIDTH), const)],
        out_specs=[out, out, out],
        out_shape=[jax.ShapeDtypeStruct((m, SB_WIDTH), BF16)] * 3,
        compiler_params=_params(("parallel",)),
        name="sb_qkv",
    )(x, gkv, gq, wkv, wq)


def _sb_kernel(q_ref, k_ref, v_ref, h_ref, wo_ref, out_ref, qm_ref, c_ref, o_ref, obf_ref):
    tq = SB_BLOCK
    iq = pl.program_id(1)
    first = lax.broadcasted_iota(jnp.int32, (tq, LANES), 1) < SB_HEAD_DIM
    ri = lax.broadcasted_iota(jnp.int32, (tq, tq), 0)
    ci = lax.broadcasted_iota(jnp.int32, (tq, tq), 1)
    upper_ext = jnp.concatenate([(ri > ci).astype(BF16), jnp.ones((tq, LANES), BF16)], axis=1)

    for p in range(SB_PAIRS):
        q2 = q_ref[:, p * LANES:(p + 1) * LANES]
        zero = jnp.zeros_like(q2)
        qm_ref[p, 0:tq, :] = jnp.where(first, q2, zero)
        qm_ref[p, tq:, :] = jnp.where(first, zero, q2)
    c_ref[...] = jnp.zeros_like(c_ref)
    o_ref[...] = jnp.zeros_like(o_ref)

    def visit(j, bias):
        start = pl.multiple_of(j * tq, tq)
        pairs = range(SB_PAIRS)
        cols = [slice(p * LANES, (p + 1) * LANES) for p in pairs]
        zs = [_dot_nt(qm_ref[p], k_ref[pl.ds(start, tq), cols[p]]) + bias for p in pairs]
        sps = [_softplus(z) for z in zs]
        accs = [_dot(sp.astype(BF16), upper_ext) for sp in sps]
        ws = []
        cmin = None
        for p in pairs:
            c = c_ref[p]
            ws.append(jnp.exp(zs[p] - sps[p] - accs[p][:, :tq] - c).astype(BF16))
            c = c + accs[p][:, tq:]
            c_ref[p] = c
            cmin = c if cmin is None else jnp.minimum(cmin, c)
        for p in pairs:
            vj = v_ref[pl.ds(start, tq), cols[p]]
            zero = jnp.zeros_like(vj)
            v2 = jnp.concatenate([jnp.where(first, vj, zero), jnp.where(first, zero, vj)], axis=0)
            o_ref[p] += _dot(jnp.concatenate([ws[p][:tq], ws[p][tq:]], axis=1), v2)
        return jnp.min(cmin)

    qpos = iq * tq + ri
    kpos = iq * tq + ci
    diag = jnp.where(jnp.logical_and(kpos < qpos, kpos >= FRONT), 0.0, SB_MASKED)
    lowest = visit(iq, jnp.concatenate([diag, diag], axis=0))

    def cond(carry):
        j, low = carry
        return jnp.logical_and(j >= 0, low < SB_SKIP)

    def body(carry):
        j, _ = carry
        kp = j * tq + lax.broadcasted_iota(jnp.int32, (1, tq), 1)
        return j - 1, visit(j, jnp.where(kp >= FRONT, 0.0, SB_MASKED))

    lax.while_loop(cond, body, (iq - 1, lowest))

    for p in range(SB_PAIRS):
        obf_ref[:, p * LANES:(p + 1) * LANES] = o_ref[p].astype(BF16)
    out_ref[...] = h_ref[...] + _dot(obf_ref[...], wo_ref[...])


def _sb_layer(h, q, k, v, wo, *, batch, lp):
    m = q.shape[0]
    nq = lp // SB_BLOCK
    tq = SB_BLOCK
    row = lambda b, i: (b * nq + i, 0)
    whole = lambda b, i: (b, 0)
    return pl.pallas_call(
        _sb_kernel,
        grid=(batch, nq),
        in_specs=[pl.BlockSpec((tq, SB_WIDTH), row),
                  pl.BlockSpec((lp, SB_WIDTH), whole, pipeline_mode=pl.Buffered(1)),
                  pl.BlockSpec((lp, SB_WIDTH), whole, pipeline_mode=pl.Buffered(1)),
                  pl.BlockSpec((tq, D_MODEL), row),
                  pl.BlockSpec((SB_WIDTH, D_MODEL), lambda b, i: (0, 0))],
        out_specs=pl.BlockSpec((tq, D_MODEL), row),
        out_shape=jax.ShapeDtypeStruct((m, D_MODEL), F32),
        scratch_shapes=[pltpu.VMEM((SB_PAIRS, 2 * tq, LANES), BF16),
                        pltpu.VMEM((SB_PAIRS, 2 * tq, tq), F32),
                        pltpu.VMEM((SB_PAIRS, tq, LANES), F32),
                        pltpu.VMEM((tq, SB_WIDTH), BF16)],
        compiler_params=_params(("parallel", "arbitrary")),
        name="sb_attn",
    )(q, k, v, h, wo)


def _final_norm_kernel(x_ref, g_ref, out_ref):
    out_ref[0] = _rms(x_ref[...], g_ref[...])


def _final_norm(x, g, *, batch, lp, seq):
    rows = max(r for r in (1024, 512, 256, 128) if seq % r == 0)
    return pl.pallas_call(
        _final_norm_kernel,
        grid=(batch, seq // rows),
        in_specs=[pl.BlockSpec((pl.Element(rows), pl.Element(D_MODEL)),
                               lambda b, r: (pl.multiple_of(b * lp + GCHUNK + r * rows, GCHUNK), 0)),
                  pl.BlockSpec((1, D_MODEL), lambda b, r: (0, 0))],
        out_specs=pl.BlockSpec((1, rows, D_MODEL), lambda b, r: (b, r, 0)),
        out_shape=jax.ShapeDtypeStruct((batch, seq, D_MODEL), F32),
        compiler_params=_params(("parallel", "parallel")),
        name="final_norm",
    )(x, g)


def _row_tile(lp):
    for tm in (640, 512, 384, 256, 128):
        if lp % tm == 0:
            return tm
    raise ValueError(f"padded length {lp} is not a multiple of {GCHUNK}")


def kernel(x, meta_tokens, gdn_norm_g, gdn_w_in, gdn_conv_w, gdn_a_log, gdn_dt_bias, gdn_onorm_g, gdn_w_out,
           kv_norm_g, w_kv, sb_norm_g, sb_w_q, sb_w_o, ffn_norm_g, ffn_w_gate_up, ffn_w_down, final_norm_g):
    batch, seq, _ = x.shape
    assert seq % GCHUNK == 0 and gdn_w_in.shape[0] == 1 and sb_w_q.shape[0] == 1
    lp = GCHUNK + seq
    tm = _row_tile(lp)

    assert tm > GCHUNK and seq >= tm
    head = jnp.concatenate([jnp.zeros((FRONT, D_MODEL), x.dtype), meta_tokens.astype(x.dtype)], axis=0)

    w_in = gdn_w_in[0]
    w_main = w_in[:, :4 * GDN_WIDTH].astype(BF16)
    w_ab = jnp.pad(w_in[:, 4 * GDN_WIDTH:], ((0, 0), (0, LANES - 2 * GDN_HEADS))).astype(BF16)
    ab_par = jnp.zeros((SUBLANES, LANES), F32)
    ab_par = ab_par.at[0, :GDN_HEADS].set(gdn_a_log[0]).at[1, :GDN_HEADS].set(gdn_dt_bias[0])

    q, k, v, gate, gb = _gdn_in(x, head, gdn_norm_g[0][None], w_main, w_ab, gdn_conv_w[0], ab_par, tm=tm, lp=lp)
    h = _gdn_core(q, k, v, gb, gate, x, head, gdn_onorm_g[0][None], gdn_w_out[0].astype(BF16))
    h = _ffn(h, ffn_norm_g[0][None], ffn_w_gate_up[0].astype(BF16), ffn_w_down[0].astype(BF16), tm=tm)

    sq, sk, sv = _qkv(h, kv_norm_g[None], sb_norm_g[0][None], w_kv.astype(BF16), sb_w_q[0].astype(BF16), tm=tm)
    h = _sb_layer(h, sq, sk, sv, sb_w_o[0].astype(BF16), batch=batch, lp=lp)
    h = _ffn(h, ffn_norm_g[1][None], ffn_w_gate_up[1].astype(BF16), ffn_w_down[1].astype(BF16), tm=tm)
    return _final_norm(h, final_norm_g[None], batch=batch, lp=lp, seq=seq)
```

```python
import functools

import jax
import jax.numpy as jnp
from jax import lax
from jax.experimental import pallas as pl
from jax.experimental.pallas import tpu as pltpu

F32 = jnp.float32
BF16 = jnp.bfloat16

D_MODEL = 1024
N_META = 16
GDN_HEADS = 8
GDN_HEAD_DIM = 128
GDN_WIDTH = GDN_HEADS * GDN_HEAD_DIM
CONV_WIDTH = 4
SB_HEADS = 16
SB_HEAD_DIM = 64
SB_WIDTH = SB_HEADS * SB_HEAD_DIM
D_FF = 2816
EPS = 1e-6

LANES = 128
SUBLANES = 8
MXU_DIM = 256
NEG_LOG2E = -1.4426950408889634
GCHUNK = 128
FRONT = GCHUNK - N_META
SB_BLOCK = 128
SB_PAIRS = SB_WIDTH // LANES
SB_SKIP = 88.0
SB_MASKED = -1e30
VMEM_LIMIT = 56 * 1024 * 1024


def _dot(a, b):
    return lax.dot_general(a, b, (((1,), (0,)), ((), ())), preferred_element_type=F32)


def _dot_nt(a, b):
    return lax.dot_general(a, b, (((1,), (1,)), ((), ())), preferred_element_type=F32)


def _split(a):
    hi = a.astype(BF16)
    lo = (a - hi.astype(F32)).astype(BF16)
    return hi, lo


def _dot_exact_lhs(m_bf16, b):
    bh, bl = _split(b)
    return _dot(m_bf16, bh) + _dot(m_bf16, bl)


def _rms(x, g):
    return x * lax.rsqrt(jnp.mean(x * x, axis=-1, keepdims=True) + EPS) * g


def _sigmoid(x):
    return 1.0 / (1.0 + jnp.exp2(x * NEG_LOG2E))


def _softplus(x):
    return jnp.maximum(x, 0.0) + jnp.log(1.0 + jnp.exp(-jnp.abs(x)))


def _params(sem):
    return pltpu.CompilerParams(dimension_semantics=sem, vmem_limit_bytes=VMEM_LIMIT)


def _gdn_in_kernel(x_ref, head_ref, ng_ref, w_ref, wab_ref, cw_ref, ab_ref,
                   q_ref, k_ref, v_ref, gate_ref, gb_ref,
                   carry_ref, *, tm):
    tb = pl.program_id(1)

    @pl.when(tb == 0)
    def _():
        carry_ref[...] = jnp.zeros_like(carry_ref)

    xt = x_ref[0]
    h = jnp.where(tb == 0, jnp.concatenate([head_ref[...], xt[:tm - GCHUNK]], axis=0), xt)
    xn = _rms(h, ng_ref[...]).astype(BF16)
    outs = (q_ref, k_ref, v_ref)

    def project(part):
        return _dot(xn, w_ref[:, part * GDN_WIDTH:(part + 1) * GDN_WIDTH])

    pending = project(0)
    for part in range(3):
        cols = slice(part * GDN_WIDTH, (part + 1) * GDN_WIDTH)
        p = pending
        pending = project(part + 1)
        ext = jnp.concatenate([carry_ref[part], p], axis=0)
        carry_ref[part] = p[tm - SUBLANES:, :]
        y = p * cw_ref[CONV_WIDTH - 1:CONV_WIDTH, cols]
        for i in range(1, CONV_WIDTH):
            y = y + pltpu.roll(ext, i, axis=0)[SUBLANES:] * cw_ref[CONV_WIDTH - 1 - i:CONV_WIDTH - i, cols]
        y = y * _sigmoid(y)
        if part < 2:
            for hd in range(GDN_HEADS):
                sl = slice(hd * GDN_HEAD_DIM, (hd + 1) * GDN_HEAD_DIM)
                seg = y[:, sl]
                outs[part][:, sl] = seg * lax.rsqrt(jnp.sum(seg * seg, axis=-1, keepdims=True) + EPS)
        else:
            v_ref[...] = y
    gate_ref[...] = pending

    ab = _dot(xn, wab_ref[...])
    a_log = ab_ref[0:1, :]
    dt_bias = ab_ref[1:2, :]
    g = -jnp.exp(a_log) * _softplus(ab + dt_bias)
    beta = _sigmoid(ab)
    lane = lax.broadcasted_iota(jnp.int32, ab.shape, 1)
    pos = tb * tm + lax.broadcasted_iota(jnp.int32, ab.shape, 0)
    gb_ref[...] = jnp.where(pos >= FRONT, jnp.where(lane < GDN_HEADS, g, beta), 0.0)


def _gdn_in(x, head, ng, w_main, w_ab, conv_w, ab_par, *, tm, lp):
    batch = x.shape[0]
    m = batch * lp
    tiles = lp // tm
    row = lambda b, t: (b * tiles + t, 0)
    const = lambda b, t: (0, 0)
    wide = pl.BlockSpec((tm, GDN_WIDTH), row)
    return pl.pallas_call(
        functools.partial(_gdn_in_kernel, tm=tm),
        grid=(batch, tiles),
        in_specs=[
            pl.BlockSpec((pl.Element(1), pl.Element(tm), pl.Element(D_MODEL)),
                         lambda b, t: (b, pl.multiple_of(jnp.maximum(t * tm - GCHUNK, 0), GCHUNK), 0)),
            pl.BlockSpec((GCHUNK, D_MODEL), const),
            pl.BlockSpec((1, D_MODEL), const),
            pl.BlockSpec((D_MODEL, 4 * GDN_WIDTH), const),
            pl.BlockSpec((D_MODEL, LANES), const),
            pl.BlockSpec((CONV_WIDTH, 3 * GDN_WIDTH), const),
            pl.BlockSpec((SUBLANES, LANES), const),
        ],
        out_specs=[wide, wide, wide, wide, pl.BlockSpec((tm, LANES), row)],
        out_shape=[jax.ShapeDtypeStruct((m, GDN_WIDTH), F32)] * 4 + [jax.ShapeDtypeStruct((m, LANES), F32)],
        scratch_shapes=[pltpu.VMEM((3, SUBLANES, GDN_WIDTH), F32)],
        compiler_params=_params(("arbitrary", "arbitrary")),
        name="gdn_in",
    )(x, head, ng, w_main, w_ab, conv_w, ab_par)


def _mm(xs, ys):
    return [_dot(x.astype(BF16), y.astype(BF16)) for x, y in zip(xs, ys)]


def _unit_lower_inverses(ns, blk_same):
    c = ns[0].shape[0]
    eye = (lax.broadcasted_iota(jnp.int32, (c, c), 0) == lax.broadcasted_iota(jnp.int32, (c, c), 1)).astype(F32)
    nd = [jnp.where(blk_same, n, 0.0) for n in ns]
    no = [n - d for n, d in zip(ns, nd)]
    nd2 = _mm(nd, nd)
    nd4 = _mm(nd2, nd2)
    nd8 = _mm(nd4, nd4)
    dinv = _mm([eye - x for x in nd], [eye + x for x in nd2])
    dinv = _mm(dinv, [eye + x for x in nd4])
    dinv = _mm(dinv, [eye + x for x in nd8])
    m1 = _mm(dinv, no)
    m2 = _mm(m1, m1)
    m4 = _mm(m2, m2)
    outer = _mm([eye - x for x in m1], [eye + x for x in m2])
    outer = _mm(outer, [eye + x for x in m4])
    t0 = _mm(outer, dinv)
    nt = [_dot(n.astype(BF16), jnp.concatenate(_split(t), axis=1)) for n, t in zip(ns, t0)]
    res = [(eye - t) - (p[:, :c] + p[:, c:]) for t, p in zip(t0, nt)]
    return [t + r for t, r in zip(t0, _mm(t0, res))]


def _gdn_core_kernel(q_ref, k_ref, v_ref, gb_ref, gate_ref, x_ref, head_ref, og_ref, wo_ref,
                     out_ref, s_ref, o_ref):
    chunk = pl.program_id(1)

    @pl.when(chunk == 0)
    def _():
        s_ref[...] = jnp.zeros_like(s_ref)

    c = GCHUNK
    ri = lax.broadcasted_iota(jnp.int32, (c, c), 0)
    ci = lax.broadcasted_iota(jnp.int32, (c, c), 1)
    incl = ri >= ci
    strict = ri > ci
    blk_same = (ri // 16) == (ci // 16)
    tril = incl.astype(BF16)
    scale = GDN_HEAD_DIM ** -0.5
    heads = range(GDN_HEADS)
    sls = [slice(hd * GDN_HEAD_DIM, (hd + 1) * GDN_HEAD_DIM) for hd in heads]
    ks = [k_ref[:, sl] for sl in sls]
    kbs = [k.astype(BF16) for k in ks]
    bmats = [jnp.broadcast_to(gb_ref[:, GDN_HEADS + hd:GDN_HEADS + hd + 1], (c, c)) for hd in heads]
    gcols = [_dot_exact_lhs(tril, jnp.broadcast_to(gb_ref[:, hd:hd + 1], (c, c))) for hd in heads]
    kks = [_dot_nt(kb, kb) for kb in kbs]
    decays = [jnp.where(incl, jnp.exp(jnp.where(incl, g - g.T, 0.0)), 0.0) for g in gcols]
    ns = [jnp.where(strict, b * d * kk, 0.0) for b, d, kk in zip(bmats, decays, kks)]
    ts = _unit_lower_inverses(ns, blk_same)
    egs = [jnp.exp(g) for g in gcols]
    rhs = [jnp.concatenate([b * v_ref[:, sl], (b * eg) * k], axis=1).astype(BF16)
           for b, eg, k, sl in zip(bmats, egs, ks, sls)]
    sols = [_dot(t.astype(BF16), r) for t, r in zip(ts, rhs)]
    qs = [q_ref[:, sl] * scale for sl in sls]
    ats = [_dot_nt(q.astype(BF16), kb) * d for q, kb, d in zip(qs, kbs, decays)]
    glasts = [g[c - 1:c, :] for g in gcols]
    wcs = [s[:, GDN_HEAD_DIM:].astype(BF16) for s in sols]
    qds = [(q * eg).astype(BF16) for q, eg in zip(qs, egs)]
    kdts = [(k * jnp.exp(gl - g)).T.astype(BF16) for k, gl, g in zip(ks, glasts, gcols)]
    atbs = [a.astype(BF16) for a in ats]

    ss = [s_ref[hd] for hd in heads]
    sbs = [s.astype(BF16) for s in ss]
    ps = [_dot(jnp.concatenate([wc, qd], axis=0), sb) for wc, qd, sb in zip(wcs, qds, sbs)]
    ubs = [(sol[:, :GDN_HEAD_DIM] - p[:c]).astype(BF16) for sol, p in zip(sols, ps)]
    rs = [_dot(jnp.concatenate([at, kdt], axis=0), ub) for at, kdt, ub in zip(atbs, kdts, ubs)]
    for hd in heads:
        s_ref[hd] = ss[hd] * jnp.exp(glasts[hd]) + rs[hd][c:]
    for hd in heads:
        o = ps[hd][c:] + rs[hd][:c]
        o = o * lax.rsqrt(jnp.mean(o * o, axis=-1, keepdims=True) + EPS) * og_ref[...]
        gate = gate_ref[:, sls[hd]]
        o_ref[:, sls[hd]] = (o * (gate * _sigmoid(gate))).astype(BF16)
    resid = jnp.where(chunk == 0, head_ref[...], x_ref[0])
    out_ref[...] = resid + _dot(o_ref[...], wo_ref[...])


def _gdn_core(q, k, v, gb, gate, x, head, og, wo):
    m = q.shape[0]
    batch = x.shape[0]
    nch = m // GCHUNK // batch
    row = lambda b, c: (b * nch + c, 0)
    const = lambda b, c: (0, 0)
    wide = pl.BlockSpec((GCHUNK, GDN_WIDTH), row)
    return pl.pallas_call(
        _gdn_core_kernel,
        grid=(batch, nch),
        in_specs=[wide, wide, wide,
                  pl.BlockSpec((GCHUNK, LANES), row),
                  wide,
                  pl.BlockSpec((1, GCHUNK, D_MODEL), lambda b, c: (b, jnp.maximum(c - 1, 0), 0)),
                  pl.BlockSpec((GCHUNK, D_MODEL), const),
                  pl.BlockSpec((1, GDN_HEAD_DIM), const),
                  pl.BlockSpec((GDN_WIDTH, D_MODEL), const)],
        out_specs=pl.BlockSpec((GCHUNK, D_MODEL), row),
        out_shape=jax.ShapeDtypeStruct((m, D_MODEL), F32),
        scratch_shapes=[pltpu.VMEM((GDN_HEADS, GDN_HEAD_DIM, GDN_HEAD_DIM), F32),
                        pltpu.VMEM((GCHUNK, GDN_WIDTH), BF16)],
        compiler_params=_params(("arbitrary", "arbitrary")),
        name="gdn_core",
    )(q, k, v, gb, gate, x, head, og, wo)


def _ffn_chunks():
    step = 3 * MXU_DIM
    return [(lo, min(lo + step, D_FF)) for lo in range(0, D_FF, step)]


def _ffn_kernel(x_ref, g_ref, wgu_ref, wd_ref, out_ref):
    x = x_ref[...]
    xn = _rms(x, g_ref[...]).astype(BF16)

    def gate_up(lo, hi):
        return _dot(xn, wgu_ref[:, lo:hi]), _dot(xn, wgu_ref[:, D_FF + lo:D_FF + hi])

    chunks = _ffn_chunks()
    acc = x
    pending = gate_up(*chunks[0])
    for i, (lo, hi) in enumerate(chunks):
        gate, up = pending
        if i + 1 < len(chunks):
            pending = gate_up(*chunks[i + 1])
        act = (gate * _sigmoid(gate) * up).astype(BF16)
        acc = acc + _dot(act, wd_ref[lo:hi, :])
    out_ref[...] = acc


def _ffn(x, g, w_gu, w_d, *, tm):
    m = x.shape[0]
    row = lambda r: (r, 0)
    const = lambda r: (0, 0)
    return pl.pallas_call(
        _ffn_kernel,
        grid=(m // tm,),
        in_specs=[
            pl.BlockSpec((tm, D_MODEL), row),
            pl.BlockSpec((1, D_MODEL), const),
            pl.BlockSpec((D_MODEL, 2 * D_FF), const, pipeline_mode=pl.Buffered(1)),
            pl.BlockSpec((D_FF, D_MODEL), const, pipeline_mode=pl.Buffered(1)),
        ],
        out_specs=pl.BlockSpec((tm, D_MODEL), row),
        out_shape=jax.ShapeDtypeStruct((m, D_MODEL), F32),
        compiler_params=_params(("parallel",)),
        name="ffn",
    )(x, g, w_gu, w_d)


def _qkv_kernel(x_ref, gkv_ref, gq_ref, wkv_ref, wq_ref, q_ref, k_ref, v_ref):
    x = x_ref[...]
    inv = lax.rsqrt(jnp.mean(x * x, axis=-1, keepdims=True) + EPS)
    xkv = (x * inv * gkv_ref[...]).astype(BF16)
    xq = (x * inv * gq_ref[...]).astype(BF16)
    k_ref[...] = _dot(xkv, wkv_ref[:, :SB_WIDTH]).astype(BF16)
    v_ref[...] = _dot(xkv, wkv_ref[:, SB_WIDTH:]).astype(BF16)
    q_ref[...] = (_dot(xq, wq_ref[...]) * (SB_HEAD_DIM ** -0.5)).astype(BF16)


def _qkv(x, gkv, gq, wkv, wq, *, tm):
    m = x.shape[0]
    row = lambda r: (r, 0)
    const = lambda r: (0, 0)
    out = pl.BlockSpec((tm, SB_WIDTH), row)
    return pl.pallas_call(
        _qkv_kernel,
        grid=(m // tm,),
        in_specs=[pl.BlockSpec((tm, D_MODEL), row),
                  pl.BlockSpec((1, D_MODEL), const),
                  pl.BlockSpec((1, D_MODEL), const),
                  pl.BlockSpec((D_MODEL, 2 * SB_WIDTH), const),
                  pl.BlockSpec((D_MODEL, SB_WIDTH), const)],
        out_specs=[out, out, out],
        out_shape=[jax.ShapeDtypeStruct((m, SB_WIDTH), BF16)] * 3,
        compiler_params=_params(("parallel",)),
        name="sb_qkv",
    )(x, gkv, gq, wkv, wq)


def _sb_kernel(q_ref, k_ref, v_ref, h_ref, wo_ref, out_ref, qm_ref, c_ref, o_ref, obf_ref):
    tq = SB_BLOCK
    iq = pl.program_id(1)
    first = lax.broadcasted_iota(jnp.int32, (tq, LANES), 1) < SB_HEAD_DIM
    ri = lax.broadcasted_iota(jnp.int32, (tq, tq), 0)
    ci = lax.broadcasted_iota(jnp.int32, (tq, tq), 1)
    upper_ext = jnp.concatenate([(ri > ci).astype(BF16), jnp.ones((tq, LANES), BF16)], axis=1)

    for p in range(SB_PAIRS):
        q2 = q_ref[:, p * LANES:(p + 1) * LANES]
        zero = jnp.zeros_like(q2)
        qm_ref[p, 0:tq, :] = jnp.where(first, q2, zero)
        qm_ref[p, tq:, :] = jnp.where(first, zero, q2)
    c_ref[...] = jnp.zeros_like(c_ref)
    o_ref[...] = jnp.zeros_like(o_ref)

    def visit(j, bias):
        start = pl.multiple_of(j * tq, tq)
        pairs = range(SB_PAIRS)
        cols = [slice(p * LANES, (p + 1) * LANES) for p in pairs]
        zs = [_dot_nt(qm_ref[p], k_ref[pl.ds(start, tq), cols[p]]) + bias for p in pairs]
        sps = [_softplus(z) for z in zs]
        accs = [_dot(sp.astype(BF16), upper_ext) for sp in sps]
        ws = []
        cmin = None
        for p in pairs:
            c = c_ref[p]
            ws.append(jnp.exp(zs[p] - sps[p] - accs[p][:, :tq] - c).astype(BF16))
            c = c + accs[p][:, tq:]
            c_ref[p] = c
            cmin = c if cmin is None else jnp.minimum(cmin, c)
        for p in pairs:
            vj = v_ref[pl.ds(start, tq), cols[p]]
            zero = jnp.zeros_like(vj)
            v2 = jnp.concatenate([jnp.where(first, vj, zero), jnp.where(first, zero, vj)], axis=0)
            o_ref[p] += _dot(jnp.concatenate([ws[p][:tq], ws[p][tq:]], axis=1), v2)
        return jnp.min(cmin)

    qpos = iq * tq + ri
    kpos = iq * tq + ci
    diag = jnp.where(jnp.logical_and(kpos < qpos, kpos >= FRONT), 0.0, SB_MASKED)
    visit(iq, jnp.concatenate([diag, diag], axis=0))

    def row_bias(j, live):
        kp = j * tq + lax.broadcasted_iota(jnp.int32, (1, tq), 1)
        return jnp.where(jnp.logical_and(kp >= FRONT, live), 0.0, SB_MASKED)

    prev = jnp.maximum(iq - 1, 0)
    lowest = visit(prev, row_bias(prev, iq >= 1))

    def cond(carry):
        j, low = carry
        return jnp.logical_and(j >= 0, low < SB_SKIP)

    def body(carry):
        j, _ = carry
        return j - 1, visit(j, row_bias(j, True))

    lax.while_loop(cond, body, (iq - 2, lowest))

    for p in range(SB_PAIRS):
        obf_ref[:, p * LANES:(p + 1) * LANES] = o_ref[p].astype(BF16)
    out_ref[...] = h_ref[...] + _dot(obf_ref[...], wo_ref[...])


def _sb_layer(h, q, k, v, wo, *, batch, lp):
    m = q.shape[0]
    nq = lp // SB_BLOCK
    tq = SB_BLOCK
    row = lambda b, i: (b * nq + i, 0)
    whole = lambda b, i: (b, 0)
    return pl.pallas_call(
        _sb_kernel,
        grid=(batch, nq),
        in_specs=[pl.BlockSpec((tq, SB_WIDTH), row),
                  pl.BlockSpec((lp, SB_WIDTH), whole, pipeline_mode=pl.Buffered(1)),
                  pl.BlockSpec((lp, SB_WIDTH), whole, pipeline_mode=pl.Buffered(1)),
                  pl.BlockSpec((tq, D_MODEL), row),
                  pl.BlockSpec((SB_WIDTH, D_MODEL), lambda b, i: (0, 0))],
        out_specs=pl.BlockSpec((tq, D_MODEL), row),
        out_shape=jax.ShapeDtypeStruct((m, D_MODEL), F32),
        scratch_shapes=[pltpu.VMEM((SB_PAIRS, 2 * tq, LANES), BF16),
                        pltpu.VMEM((SB_PAIRS, 2 * tq, tq), F32),
                        pltpu.VMEM((SB_PAIRS, tq, LANES), F32),
                        pltpu.VMEM((tq, SB_WIDTH), BF16)],
        compiler_params=_params(("parallel", "arbitrary")),
        name="sb_attn",
    )(q, k, v, h, wo)


def _final_norm_kernel(x_ref, g_ref, out_ref):
    out_ref[0] = _rms(x_ref[...], g_ref[...])


def _final_norm(x, g, *, batch, lp, seq):
    rows = max(r for r in (1024, 512, 256, 128) if seq % r == 0)
    return pl.pallas_call(
        _final_norm_kernel,
        grid=(batch, seq // rows),
        in_specs=[pl.BlockSpec((pl.Element(rows), pl.Element(D_MODEL)),
                               lambda b, r: (pl.multiple_of(b * lp + GCHUNK + r * rows, GCHUNK), 0)),
                  pl.BlockSpec((1, D_MODEL), lambda b, r: (0, 0))],
        out_specs=pl.BlockSpec((1, rows, D_MODEL), lambda b, r: (b, r, 0)),
        out_shape=jax.ShapeDtypeStruct((batch, seq, D_MODEL), F32),
        compiler_params=_params(("parallel", "parallel")),
        name="final_norm",
    )(x, g)


def _row_tile(lp):
    for tm in (640, 512, 384, 256, 128):
        if lp % tm == 0:
            return tm
    raise ValueError(f"padded length {lp} is not a multiple of {GCHUNK}")


def kernel(x, meta_tokens, gdn_norm_g, gdn_w_in, gdn_conv_w, gdn_a_log, gdn_dt_bias, gdn_onorm_g, gdn_w_out,
           kv_norm_g, w_kv, sb_norm_g, sb_w_q, sb_w_o, ffn_norm_g, ffn_w_gate_up, ffn_w_down, final_norm_g):
    batch, seq, _ = x.shape
    assert seq % GCHUNK == 0 and gdn_w_in.shape[0] == 1 and sb_w_q.shape[0] == 1
    lp = GCHUNK + seq
    tm = _row_tile(lp)

    assert tm > GCHUNK and seq >= tm
    head = jnp.concatenate([jnp.zeros((FRONT, D_MODEL), x.dtype), meta_tokens.astype(x.dtype)], axis=0)

    w_in = gdn_w_in[0]
    w_main = w_in[:, :4 * GDN_WIDTH].astype(BF16)
    w_ab = jnp.pad(w_in[:, 4 * GDN_WIDTH:], ((0, 0), (0, LANES - 2 * GDN_HEADS))).astype(BF16)
    ab_par = jnp.zeros((SUBLANES, LANES), F32)
    ab_par = ab_par.at[0, :GDN_HEADS].set(gdn_a_log[0]).at[1, :GDN_HEADS].set(gdn_dt_bias[0])

    q, k, v, gate, gb = _gdn_in(x, head, gdn_norm_g[0][None], w_main, w_ab, gdn_conv_w[0], ab_par, tm=tm, lp=lp)
    h = _gdn_core(q, k, v, gb, gate, x, head, gdn_onorm_g[0][None], gdn_w_out[0].astype(BF16))
    h = _ffn(h, ffn_norm_g[0][None], ffn_w_gate_up[0].astype(BF16), ffn_w_down[0].astype(BF16), tm=tm)

    sq, sk, sv = _qkv(h, kv_norm_g[None], sb_norm_g[0][None], w_kv.astype(BF16), sb_w_q[0].astype(BF16), tm=tm)
    h = _sb_layer(h, sq, sk, sv, sb_w_o[0].astype(BF16), batch=batch, lp=lp)
    h = _ffn(h, ffn_norm_g[1][None], ffn_w_gate_up[1].astype(BF16), ffn_w_down[1].astype(BF16), tm=tm)
    return _final_norm(h, final_norm_g[None], batch=batch, lp=lp, seq=seq)
```

```python
import functools

import jax
import jax.numpy as jnp
from jax import lax
from jax.experimental import pallas as pl
from jax.experimental.pallas import tpu as pltpu

F32 = jnp.float32
BF16 = jnp.bfloat16

D_MODEL = 1024
N_META = 16
GDN_HEADS = 8
GDN_HEAD_DIM = 128
GDN_WIDTH = GDN_HEADS * GDN_HEAD_DIM
CONV_WIDTH = 4
SB_HEADS = 16
SB_HEAD_DIM = 64
SB_WIDTH = SB_HEADS * SB_HEAD_DIM
D_FF = 2816
EPS = 1e-6

LANES = 128
SUBLANES = 8
MXU_DIM = 256
NEG_LOG2E = -1.4426950408889634
GCHUNK = 128
FRONT = GCHUNK - N_META
SB_BLOCK = 128
SB_PAIRS = SB_WIDTH // LANES
SB_SKIP = 88.0
SB_MASKED = -1e30
VMEM_LIMIT = 56 * 1024 * 1024


def _dot(a, b):
    return lax.dot_general(a, b, (((1,), (0,)), ((), ())), preferred_element_type=F32)


def _dot_nt(a, b):
    return lax.dot_general(a, b, (((1,), (1,)), ((), ())), preferred_element_type=F32)


def _split(a):
    hi = a.astype(BF16)
    lo = (a - hi.astype(F32)).astype(BF16)
    return hi, lo


def _dot_exact_lhs(m_bf16, b):
    bh, bl = _split(b)
    return _dot(m_bf16, bh) + _dot(m_bf16, bl)


def _rms(x, g):
    return x * lax.rsqrt(jnp.mean(x * x, axis=-1, keepdims=True) + EPS) * g


def _sigmoid(x):
    return 1.0 / (1.0 + jnp.exp2(x * NEG_LOG2E))


def _softplus(x):
    return jnp.maximum(x, 0.0) + jnp.log(1.0 + jnp.exp(-jnp.abs(x)))


def _params(sem):
    return pltpu.CompilerParams(dimension_semantics=sem, vmem_limit_bytes=VMEM_LIMIT)


def _gdn_in_kernel(x_ref, head_ref, ng_ref, w_ref, wab_ref, cw_ref, ab_ref,
                   q_ref, k_ref, v_ref, gate_ref, gb_ref,
                   carry_ref, *, tm):
    tb = pl.program_id(1)

    @pl.when(tb == 0)
    def _():
        carry_ref[...] = jnp.zeros_like(carry_ref)

    xt = x_ref[0]
    h = jnp.where(tb == 0, jnp.concatenate([head_ref[...], xt[:tm - GCHUNK]], axis=0), xt)
    xn = _rms(h, ng_ref[...]).astype(BF16)
    outs = (q_ref, k_ref, v_ref)

    def project(part):
        return _dot(xn, w_ref[:, part * GDN_WIDTH:(part + 1) * GDN_WIDTH])

    pending = project(0)
    for part in range(3):
        cols = slice(part * GDN_WIDTH, (part + 1) * GDN_WIDTH)
        p = pending
        pending = project(part + 1)
        ext = jnp.concatenate([carry_ref[part], p], axis=0)
        carry_ref[part] = p[tm - SUBLANES:, :]
        w0, w1, w2, w3 = (cw_ref[i:i + 1, cols] for i in range(CONV_WIDTH))
        prev = pltpu.roll(ext, 1, axis=0)
        u = ext * w1 + prev * w0
        y = p * w3 + prev[SUBLANES:] * w2 + pltpu.roll(u, 2, axis=0)[SUBLANES:]
        y = y * _sigmoid(y)
        if part < 2:
            for hd in range(GDN_HEADS):
                sl = slice(hd * GDN_HEAD_DIM, (hd + 1) * GDN_HEAD_DIM)
                seg = y[:, sl]
                outs[part][:, sl] = seg * lax.rsqrt(jnp.sum(seg * seg, axis=-1, keepdims=True) + EPS)
        else:
            v_ref[...] = y
    gate_ref[...] = pending

    ab = _dot(xn, wab_ref[...])
    a_log = ab_ref[0:1, :]
    dt_bias = ab_ref[1:2, :]
    g = -jnp.exp(a_log) * _softplus(ab + dt_bias)
    beta = _sigmoid(ab)
    lane = lax.broadcasted_iota(jnp.int32, ab.shape, 1)
    pos = tb * tm + lax.broadcasted_iota(jnp.int32, ab.shape, 0)
    gb_ref[...] = jnp.where(pos >= FRONT, jnp.where(lane < GDN_HEADS, g, beta), 0.0)


def _gdn_in(x, head, ng, w_main, w_ab, conv_w, ab_par, *, tm, lp):
    batch = x.shape[0]
    m = batch * lp
    tiles = lp // tm
    row = lambda b, t: (b * tiles + t, 0)
    const = lambda b, t: (0, 0)
    wide = pl.BlockSpec((tm, GDN_WIDTH), row)
    return pl.pallas_call(
        functools.partial(_gdn_in_kernel, tm=tm),
        grid=(batch, tiles),
        in_specs=[
            pl.BlockSpec((pl.Element(1), pl.Element(tm), pl.Element(D_MODEL)),
                         lambda b, t: (b, pl.multiple_of(jnp.maximum(t * tm - GCHUNK, 0), GCHUNK), 0)),
            pl.BlockSpec((GCHUNK, D_MODEL), const),
            pl.BlockSpec((1, D_MODEL), const),
            pl.BlockSpec((D_MODEL, 4 * GDN_WIDTH), const),
            pl.BlockSpec((D_MODEL, LANES), const),
            pl.BlockSpec((CONV_WIDTH, 3 * GDN_WIDTH), const),
            pl.BlockSpec((SUBLANES, LANES), const),
        ],
        out_specs=[wide, wide, wide, wide, pl.BlockSpec((tm, LANES), row)],
        out_shape=[jax.ShapeDtypeStruct((m, GDN_WIDTH), F32)] * 4 + [jax.ShapeDtypeStruct((m, LANES), F32)],
        scratch_shapes=[pltpu.VMEM((3, SUBLANES, GDN_WIDTH), F32)],
        compiler_params=_params(("arbitrary", "arbitrary")),
        name="gdn_in",
    )(x, head, ng, w_main, w_ab, conv_w, ab_par)


def _mm(xs, ys):
    return [_dot(x.astype(BF16), y.astype(BF16)) for x, y in zip(xs, ys)]


def _unit_lower_inverses(ns, blk_same):
    c = ns[0].shape[0]
    eye = (lax.broadcasted_iota(jnp.int32, (c, c), 0) == lax.broadcasted_iota(jnp.int32, (c, c), 1)).astype(F32)
    nd = [jnp.where(blk_same, n, 0.0) for n in ns]
    no = [n - d for n, d in zip(ns, nd)]
    nd2 = _mm(nd, nd)
    nd4 = _mm(nd2, nd2)
    nd8 = _mm(nd4, nd4)
    dinv = _mm([eye - x for x in nd], [eye + x for x in nd2])
    dinv = _mm(dinv, [eye + x for x in nd4])
    dinv = _mm(dinv, [eye + x for x in nd8])
    m1 = _mm(dinv, no)
    m2 = _mm(m1, m1)
    m4 = _mm(m2, m2)
    outer = _mm([eye - x for x in m1], [eye + x for x in m2])
    outer = _mm(outer, [eye + x for x in m4])
    t0 = _mm(outer, dinv)
    nt = [_dot(n.astype(BF16), jnp.concatenate(_split(t), axis=1)) for n, t in zip(ns, t0)]
    res = [(eye - t) - (p[:, :c] + p[:, c:]) for t, p in zip(t0, nt)]
    return [t + r for t, r in zip(t0, _mm(t0, res))]


def _gdn_core_kernel(q_ref, k_ref, v_ref, gb_ref, gate_ref, x_ref, head_ref, og_ref, wo_ref,
                     out_ref, s_ref, o_ref):
    chunk = pl.program_id(1)

    @pl.when(chunk == 0)
    def _():
        s_ref[...] = jnp.zeros_like(s_ref)

    c = GCHUNK
    ri = lax.broadcasted_iota(jnp.int32, (c, c), 0)
    ci = lax.broadcasted_iota(jnp.int32, (c, c), 1)
    incl = ri >= ci
    strict = ri > ci
    blk_same = (ri // 16) == (ci // 16)
    tril = incl.astype(BF16)
    scale = GDN_HEAD_DIM ** -0.5
    heads = range(GDN_HEADS)
    sls = [slice(hd * GDN_HEAD_DIM, (hd + 1) * GDN_HEAD_DIM) for hd in heads]
    ks = [k_ref[:, sl] for sl in sls]
    kbs = [k.astype(BF16) for k in ks]
    bmats = [jnp.broadcast_to(gb_ref[:, GDN_HEADS + hd:GDN_HEADS + hd + 1], (c, c)) for hd in heads]
    gcols = [_dot_exact_lhs(tril, jnp.broadcast_to(gb_ref[:, hd:hd + 1], (c, c))) for hd in heads]
    kks = [_dot_nt(kb, kb) for kb in kbs]
    decays = [jnp.where(incl, jnp.exp(jnp.where(incl, g - g.T, 0.0)), 0.0) for g in gcols]
    ns = [jnp.where(strict, b * d * kk, 0.0) for b, d, kk in zip(bmats, decays, kks)]
    ts = _unit_lower_inverses(ns, blk_same)
    egs = [jnp.exp(g) for g in gcols]
    rhs = [jnp.concatenate([b * v_ref[:, sl], (b * eg) * k], axis=1).astype(BF16)
           for b, eg, k, sl in zip(bmats, egs, ks, sls)]
    sols = [_dot(t.astype(BF16), r) for t, r in zip(ts, rhs)]
    qs = [q_ref[:, sl] * scale for sl in sls]
    ats = [_dot_nt(q.astype(BF16), kb) * d for q, kb, d in zip(qs, kbs, decays)]
    glasts = [g[c - 1:c, :] for g in gcols]
    wcs = [s[:, GDN_HEAD_DIM:].astype(BF16) for s in sols]
    qds = [(q * eg).astype(BF16) for q, eg in zip(qs, egs)]
    kdts = [(k * jnp.exp(gl - g)).T.astype(BF16) for k, gl, g in zip(ks, glasts, gcols)]
    atbs = [a.astype(BF16) for a in ats]

    ss = [s_ref[hd] for hd in heads]
    sbs = [s.astype(BF16) for s in ss]
    ps = [_dot(jnp.concatenate([wc, qd], axis=0), sb) for wc, qd, sb in zip(wcs, qds, sbs)]
    ubs = [(sol[:, :GDN_HEAD_DIM] - p[:c]).astype(BF16) for sol, p in zip(sols, ps)]
    rs = [_dot(jnp.concatenate([at, kdt], axis=0), ub) for at, kdt, ub in zip(atbs, kdts, ubs)]
    for hd in heads:
        s_ref[hd] = ss[hd] * jnp.exp(glasts[hd]) + rs[hd][c:]
    for hd in heads:
        o = ps[hd][c:] + rs[hd][:c]
        o = o * lax.rsqrt(jnp.mean(o * o, axis=-1, keepdims=True) + EPS) * og_ref[...]
        gate = gate_ref[:, sls[hd]]
        o_ref[:, sls[hd]] = (o * (gate * _sigmoid(gate))).astype(BF16)
    resid = jnp.where(chunk == 0, head_ref[...], x_ref[0])
    out_ref[...] = resid + _dot(o_ref[...], wo_ref[...])


def _gdn_core(q, k, v, gb, gate, x, head, og, wo):
    m = q.shape[0]
    batch = x.shape[0]
    nch = m // GCHUNK // batch
    row = lambda b, c: (b * nch + c, 0)
    const = lambda b, c: (0, 0)
    wide = pl.BlockSpec((GCHUNK, GDN_WIDTH), row)
    return pl.pallas_call(
        _gdn_core_kernel,
        grid=(batch, nch),
        in_specs=[wide, wide, wide,
                  pl.BlockSpec((GCHUNK, LANES), row),
                  wide,
                  pl.BlockSpec((1, GCHUNK, D_MODEL), lambda b, c: (b, jnp.maximum(c - 1, 0), 0)),
                  pl.BlockSpec((GCHUNK, D_MODEL), const),
                  pl.BlockSpec((1, GDN_HEAD_DIM), const),
                  pl.BlockSpec((GDN_WIDTH, D_MODEL), const)],
        out_specs=pl.BlockSpec((GCHUNK, D_MODEL), row),
        out_shape=jax.ShapeDtypeStruct((m, D_MODEL), F32),
        scratch_shapes=[pltpu.VMEM((GDN_HEADS, GDN_HEAD_DIM, GDN_HEAD_DIM), F32),
                        pltpu.VMEM((GCHUNK, GDN_WIDTH), BF16)],
        compiler_params=_params(("arbitrary", "arbitrary")),
        name="gdn_core",
    )(q, k, v, gb, gate, x, head, og, wo)


def _ffn_chunks():
    step = 3 * MXU_DIM
    return [(lo, min(lo + step, D_FF)) for lo in range(0, D_FF, step)]


def _ffn_body(x, g_ref, wgu_ref, wd_ref):
    xn = _rms(x, g_ref[...]).astype(BF16)

    def gate_up(lo, hi):
        return _dot(xn, wgu_ref[:, lo:hi]), _dot(xn, wgu_ref[:, D_FF + lo:D_FF + hi])

    chunks = _ffn_chunks()
    acc = x
    pending = gate_up(*chunks[0])
    for i, (lo, hi) in enumerate(chunks):
        gate, up = pending
        if i + 1 < len(chunks):
            pending = gate_up(*chunks[i + 1])
        act = (gate * _sigmoid(gate) * up).astype(BF16)
        acc = acc + _dot(act, wd_ref[lo:hi, :])
    return acc


def _ffn_kernel(x_ref, g_ref, wgu_ref, wd_ref, out_ref):
    out_ref[...] = _ffn_body(x_ref[...], g_ref, wgu_ref, wd_ref)


def _ffn_final_kernel(x_ref, g_ref, wgu_ref, wd_ref, gf_ref, out_ref):
    out_ref[0] = _rms(_ffn_body(x_ref[...], g_ref, wgu_ref, wd_ref), gf_ref[...])


def _ffn_weight_specs(const):
    return [pl.BlockSpec((1, D_MODEL), const),
            pl.BlockSpec((D_MODEL, 2 * D_FF), const, pipeline_mode=pl.Buffered(1)),
            pl.BlockSpec((D_FF, D_MODEL), const, pipeline_mode=pl.Buffered(1))]


def _ffn(x, g, w_gu, w_d, *, tm):
    m = x.shape[0]
    row = lambda r: (r, 0)
    return pl.pallas_call(
        _ffn_kernel,
        grid=(m // tm,),
        in_specs=[pl.BlockSpec((tm, D_MODEL), row)] + _ffn_weight_specs(lambda r: (0, 0)),
        out_specs=pl.BlockSpec((tm, D_MODEL), row),
        out_shape=jax.ShapeDtypeStruct((m, D_MODEL), F32),
        compiler_params=_params(("parallel",)),
        name="ffn",
    )(x, g, w_gu, w_d)


def _ffn_final(x, g, w_gu, w_d, gf, *, batch, lp, seq):
    rows = max(r for r in (512, 256, 128) if seq % r == 0)
    const = lambda b, r: (0, 0)
    return pl.pallas_call(
        _ffn_final_kernel,
        grid=(batch, seq // rows),
        in_specs=[pl.BlockSpec((pl.Element(rows), pl.Element(D_MODEL)),
                               lambda b, r: (pl.multiple_of(b * lp + GCHUNK + r * rows, GCHUNK), 0))]
        + _ffn_weight_specs(const) + [pl.BlockSpec((1, D_MODEL), const)],
        out_specs=pl.BlockSpec((1, rows, D_MODEL), lambda b, r: (b, r, 0)),
        out_shape=jax.ShapeDtypeStruct((batch, seq, D_MODEL), F32),
        compiler_params=_params(("parallel", "parallel")),
        name="ffn_final",
    )(x, g, w_gu, w_d, gf)


def _qkv_kernel(x_ref, gkv_ref, gq_ref, wkv_ref, wq_ref, q_ref, k_ref, v_ref):
    x = x_ref[...]
    inv = lax.rsqrt(jnp.mean(x * x, axis=-1, keepdims=True) + EPS)
    xkv = (x * inv * gkv_ref[...]).astype(BF16)
    xq = (x * inv * gq_ref[...]).astype(BF16)
    k_ref[...] = _dot(xkv, wkv_ref[:, :SB_WIDTH]).astype(BF16)
    v_ref[...] = _dot(xkv, wkv_ref[:, SB_WIDTH:]).astype(BF16)
    q_ref[...] = (_dot(xq, wq_ref[...]) * (SB_HEAD_DIM ** -0.5)).astype(BF16)


def _qkv(x, gkv, gq, wkv, wq, *, tm):
    m = x.shape[0]
    row = lambda r: (r, 0)
    const = lambda r: (0, 0)
    out = pl.BlockSpec((tm, SB_WIDTH), row)
    return pl.pallas_call(
        _qkv_kernel,
        grid=(m // tm,),
        in_specs=[pl.BlockSpec((tm, D_MODEL), row),
                  pl.BlockSpec((1, D_MODEL), const),
                  pl.BlockSpec((1, D_MODEL), const),
                  pl.BlockSpec((D_MODEL, 2 * SB_WIDTH), const),
                  pl.BlockSpec((D_MODEL, SB_WIDTH), const)],
        out_specs=[out, out, out],
        out_shape=[jax.ShapeDtypeStruct((m, SB_WIDTH), BF16)] * 3,
        compiler_params=_params(("parallel",)),
        name="sb_qkv",
    )(x, gkv, gq, wkv, wq)


def _sb_kernel(q_ref, k_ref, v_ref, h_ref, wo_ref, out_ref, qm_ref, c_ref, o_ref, obf_ref):
    tq = SB_BLOCK
    iq = pl.program_id(1)
    first = lax.broadcasted_iota(jnp.int32, (tq, LANES), 1) < SB_HEAD_DIM
    ri = lax.broadcasted_iota(jnp.int32, (tq, tq), 0)
    ci = lax.broadcasted_iota(jnp.int32, (tq, tq), 1)
    upper_ext = jnp.concatenate([(ri > ci).astype(BF16), jnp.ones((tq, LANES), BF16)], axis=1)

    for p in range(SB_PAIRS):
        q2 = q_ref[:, p * LANES:(p + 1) * LANES]
        zero = jnp.zeros_like(q2)
        qm_ref[p, 0:tq, :] = jnp.where(first, q2, zero)
        qm_ref[p, tq:, :] = jnp.where(first, zero, q2)
    c_ref[...] = jnp.zeros_like(c_ref)
    o_ref[...] = jnp.zeros_like(o_ref)

    def visit(j, bias):
        start = pl.multiple_of(j * tq, tq)
        pairs = range(SB_PAIRS)
        cols = [slice(p * LANES, (p + 1) * LANES) for p in pairs]
        zs = [_dot_nt(qm_ref[p], k_ref[pl.ds(start, tq), cols[p]]) + bias for p in pairs]
        sps = [_softplus(z) for z in zs]
        accs = [_dot(sp.astype(BF16), upper_ext) for sp in sps]
        ws = []
        cmin = None
        for p in pairs:
            c = c_ref[p]
            ws.append(jnp.exp(zs[p] - sps[p] - accs[p][:, :tq] - c).astype(BF16))
            c = c + accs[p][:, tq:]
            c_ref[p] = c
            cmin = c if cmin is None else jnp.minimum(cmin, c)
        for p in pairs:
            vj = v_ref[pl.ds(start, tq), cols[p]]
            zero = jnp.zeros_like(vj)
            v2 = jnp.concatenate([jnp.where(first, vj, zero), jnp.where(first, zero, vj)], axis=0)
            o_ref[p] += _dot(jnp.concatenate([ws[p][:tq], ws[p][tq:]], axis=1), v2)
        return jnp.min(cmin)

    qpos = iq * tq + ri
    kpos = iq * tq + ci
    diag = jnp.where(jnp.logical_and(kpos < qpos, kpos >= FRONT), 0.0, SB_MASKED)
    visit(iq, jnp.concatenate([diag, diag], axis=0))

    def row_bias(j, live):
        kp = j * tq + lax.broadcasted_iota(jnp.int32, (1, tq), 1)
        return jnp.where(jnp.logical_and(kp >= FRONT, live), 0.0, SB_MASKED)

    prev = jnp.maximum(iq - 1, 0)
    lowest = visit(prev, row_bias(prev, iq >= 1))

    def cond(carry):
        j, low = carry
        return jnp.logical_and(j >= 0, low < SB_SKIP)

    def body(carry):
        j, _ = carry
        return j - 1, visit(j, row_bias(j, True))

    lax.while_loop(cond, body, (iq - 2, lowest))

    for p in range(SB_PAIRS):
        obf_ref[:, p * LANES:(p + 1) * LANES] = o_ref[p].astype(BF16)
    out_ref[...] = h_ref[...] + _dot(obf_ref[...], wo_ref[...])


def _sb_layer(h, q, k, v, wo, *, batch, lp):
    m = q.shape[0]
    nq = lp // SB_BLOCK
    tq = SB_BLOCK
    row = lambda b, i: (b * nq + i, 0)
    whole = lambda b, i: (b, 0)
    return pl.pallas_call(
        _sb_kernel,
        grid=(batch, nq),
        in_specs=[pl.BlockSpec((tq, SB_WIDTH), row),
                  pl.BlockSpec((lp, SB_WIDTH), whole, pipeline_mode=pl.Buffered(1)),
                  pl.BlockSpec((lp, SB_WIDTH), whole, pipeline_mode=pl.Buffered(1)),
                  pl.BlockSpec((tq, D_MODEL), row),
                  pl.BlockSpec((SB_WIDTH, D_MODEL), lambda b, i: (0, 0))],
        out_specs=pl.BlockSpec((tq, D_MODEL), row),
        out_shape=jax.ShapeDtypeStruct((m, D_MODEL), F32),
        scratch_shapes=[pltpu.VMEM((SB_PAIRS, 2 * tq, LANES), BF16),
                        pltpu.VMEM((SB_PAIRS, 2 * tq, tq), F32),
                        pltpu.VMEM((SB_PAIRS, tq, LANES), F32),
                        pltpu.VMEM((tq, SB_WIDTH), BF16)],
        compiler_params=_params(("parallel", "arbitrary")),
        name="sb_attn",
    )(q, k, v, h, wo)


def _row_tile(lp):
    for tm in (640, 512, 384, 256, 128):
        if lp % tm == 0:
            return tm
    raise ValueError(f"padded length {lp} is not a multiple of {GCHUNK}")


def kernel(x, meta_tokens, gdn_norm_g, gdn_w_in, gdn_conv_w, gdn_a_log, gdn_dt_bias, gdn_onorm_g, gdn_w_out,
           kv_norm_g, w_kv, sb_norm_g, sb_w_q, sb_w_o, ffn_norm_g, ffn_w_gate_up, ffn_w_down, final_norm_g):
    batch, seq, _ = x.shape
    assert seq % GCHUNK == 0 and gdn_w_in.shape[0] == 1 and sb_w_q.shape[0] == 1
    lp = GCHUNK + seq
    tm = _row_tile(lp)

    assert tm > GCHUNK and seq >= tm
    head = jnp.concatenate([jnp.zeros((FRONT, D_MODEL), x.dtype), meta_tokens.astype(x.dtype)], axis=0)

    w_in = gdn_w_in[0]
    w_main = w_in[:, :4 * GDN_WIDTH].astype(BF16)
    w_ab = jnp.pad(w_in[:, 4 * GDN_WIDTH:], ((0, 0), (0, LANES - 2 * GDN_HEADS))).astype(BF16)
    ab_par = jnp.zeros((SUBLANES, LANES), F32)
    ab_par = ab_par.at[0, :GDN_HEADS].set(gdn_a_log[0]).at[1, :GDN_HEADS].set(gdn_dt_bias[0])

    q, k, v, gate, gb = _gdn_in(x, head, gdn_norm_g[0][None], w_main, w_ab, gdn_conv_w[0], ab_par, tm=tm, lp=lp)
    h = _gdn_core(q, k, v, gb, gate, x, head, gdn_onorm_g[0][None], gdn_w_out[0].astype(BF16))
    h = _ffn(h, ffn_norm_g[0][None], ffn_w_gate_up[0].astype(BF16), ffn_w_down[0].astype(BF16), tm=tm)

    sq, sk, sv = _qkv(h, kv_norm_g[None], sb_norm_g[0][None], w_kv.astype(BF16), sb_w_q[0].astype(BF16), tm=tm)
    h = _sb_layer(h, sq, sk, sv, sb_w_o[0].astype(BF16), batch=batch, lp=lp)
    return _ffn_final(h, ffn_norm_g[1][None], ffn_w_gate_up[1].astype(BF16), ffn_w_down[1].astype(BF16),
                      final_norm_g[None], batch=batch, lp=lp, seq=seq)
```

```python
import functools

import jax
import jax.numpy as jnp
from jax import lax
from jax.experimental import pallas as pl
from jax.experimental.pallas import tpu as pltpu

F32 = jnp.float32
BF16 = jnp.bfloat16

D_MODEL = 1024
N_META = 16
GDN_HEADS = 8
GDN_HEAD_DIM = 128
GDN_WIDTH = GDN_HEADS * GDN_HEAD_DIM
CONV_WIDTH = 4
SB_HEADS = 16
SB_HEAD_DIM = 64
SB_WIDTH = SB_HEADS * SB_HEAD_DIM
D_FF = 2816
EPS = 1e-6

LANES = 128
SUBLANES = 8
MXU_DIM = 256
NEG_LOG2E = -1.4426950408889634
GCHUNK = 128
FRONT = GCHUNK - N_META
SB_BLOCK = 128
SB_PAIRS = SB_WIDTH // LANES
SB_SKIP = 88.0
SB_MASKED = -1e30
VMEM_LIMIT = 60 * 1024 * 1024


def _dot(a, b):
    return lax.dot_general(a, b, (((1,), (0,)), ((), ())), preferred_element_type=F32)


def _dot_nt(a, b):
    return lax.dot_general(a, b, (((1,), (1,)), ((), ())), preferred_element_type=F32)


def _split(a):
    hi = a.astype(BF16)
    lo = (a - hi.astype(F32)).astype(BF16)
    return hi, lo


def _dot_exact_lhs(m_bf16, b):
    bh, bl = _split(b)
    return _dot(m_bf16, bh) + _dot(m_bf16, bl)


def _rms(x, g):
    return x * lax.rsqrt(jnp.mean(x * x, axis=-1, keepdims=True) + EPS) * g


def _sigmoid(x):
    return 1.0 / (1.0 + jnp.exp2(x * NEG_LOG2E))


def _softplus(x):
    return jnp.maximum(x, 0.0) + jnp.log(1.0 + jnp.exp(-jnp.abs(x)))


def _params(sem):
    return pltpu.CompilerParams(dimension_semantics=sem, vmem_limit_bytes=VMEM_LIMIT)


def _gdn_in_kernel(x_ref, head_ref, ng_ref, w_ref, wab_ref, cw_ref, ab_ref,
                   q_ref, k_ref, v_ref, gate_ref, gb_ref,
                   carry_ref, *, tm):
    tb = pl.program_id(1)

    @pl.when(tb == 0)
    def _():
        carry_ref[...] = jnp.zeros_like(carry_ref)

    xt = x_ref[0]
    h = jnp.where(tb == 0, jnp.concatenate([head_ref[...], xt[:tm - GCHUNK]], axis=0), xt)
    xn = _rms(h, ng_ref[...]).astype(BF16)
    outs = (q_ref, k_ref, v_ref)

    def project(part):
        return _dot(xn, w_ref[:, part * GDN_WIDTH:(part + 1) * GDN_WIDTH])

    pending = project(0)
    for part in range(3):
        cols = slice(part * GDN_WIDTH, (part + 1) * GDN_WIDTH)
        p = pending
        pending = project(part + 1)
        ext = jnp.concatenate([carry_ref[part], p], axis=0)
        carry_ref[part] = p[tm - SUBLANES:, :]
        w0, w1, w2, w3 = (cw_ref[i:i + 1, cols] for i in range(CONV_WIDTH))
        prev = pltpu.roll(ext, 1, axis=0)
        u = ext * w1 + prev * w0
        y = p * w3 + prev[SUBLANES:] * w2 + pltpu.roll(u, 2, axis=0)[SUBLANES:]
        y = y * _sigmoid(y)
        if part < 2:
            for hd in range(GDN_HEADS):
                sl = slice(hd * GDN_HEAD_DIM, (hd + 1) * GDN_HEAD_DIM)
                seg = y[:, sl]
                outs[part][:, sl] = seg * lax.rsqrt(jnp.sum(seg * seg, axis=-1, keepdims=True) + EPS)
        else:
            v_ref[...] = y
    gate_ref[...] = pending

    ab = _dot(xn, wab_ref[...])
    a_log = ab_ref[0:1, :]
    dt_bias = ab_ref[1:2, :]
    g = -jnp.exp(a_log) * _softplus(ab + dt_bias)
    beta = _sigmoid(ab)
    lane = lax.broadcasted_iota(jnp.int32, ab.shape, 1)
    pos = tb * tm + lax.broadcasted_iota(jnp.int32, ab.shape, 0)
    gb_ref[...] = jnp.where(pos >= FRONT, jnp.where(lane < GDN_HEADS, g, beta), 0.0)


def _gdn_in(x, head, ng, w_main, w_ab, conv_w, ab_par, *, tm, lp):
    batch = x.shape[0]
    m = batch * lp
    tiles = lp // tm
    row = lambda b, t: (b * tiles + t, 0)
    const = lambda b, t: (0, 0)
    wide = pl.BlockSpec((tm, GDN_WIDTH), row)
    return pl.pallas_call(
        functools.partial(_gdn_in_kernel, tm=tm),
        grid=(batch, tiles),
        in_specs=[
            pl.BlockSpec((pl.Element(1), pl.Element(tm), pl.Element(D_MODEL)),
                         lambda b, t: (b, pl.multiple_of(jnp.maximum(t * tm - GCHUNK, 0), GCHUNK), 0)),
            pl.BlockSpec((GCHUNK, D_MODEL), const),
            pl.BlockSpec((1, D_MODEL), const),
            pl.BlockSpec((D_MODEL, 4 * GDN_WIDTH), const),
            pl.BlockSpec((D_MODEL, LANES), const),
            pl.BlockSpec((CONV_WIDTH, 3 * GDN_WIDTH), const),
            pl.BlockSpec((SUBLANES, LANES), const),
        ],
        out_specs=[wide, wide, wide, wide, pl.BlockSpec((tm, LANES), row)],
        out_shape=[jax.ShapeDtypeStruct((m, GDN_WIDTH), F32)] * 4 + [jax.ShapeDtypeStruct((m, LANES), F32)],
        scratch_shapes=[pltpu.VMEM((3, SUBLANES, GDN_WIDTH), F32)],
        compiler_params=_params(("arbitrary", "arbitrary")),
        name="gdn_in",
    )(x, head, ng, w_main, w_ab, conv_w, ab_par)


def _mm(xs, ys):
    return [_dot(x.astype(BF16), y.astype(BF16)) for x, y in zip(xs, ys)]


def _unit_lower_inverses(ns, blk_same):
    c = ns[0].shape[0]
    eye = (lax.broadcasted_iota(jnp.int32, (c, c), 0) == lax.broadcasted_iota(jnp.int32, (c, c), 1)).astype(F32)
    nd = [jnp.where(blk_same, n, 0.0) for n in ns]
    no = [n - d for n, d in zip(ns, nd)]
    nd2 = _mm(nd, nd)
    nd4 = _mm(nd2, nd2)
    nd8 = _mm(nd4, nd4)
    dinv = _mm([eye - x for x in nd], [eye + x for x in nd2])
    dinv = _mm(dinv, [eye + x for x in nd4])
    dinv = _mm(dinv, [eye + x for x in nd8])
    m1 = _mm(dinv, no)
    m2 = _mm(m1, m1)
    m4 = _mm(m2, m2)
    outer = _mm([eye - x for x in m1], [eye + x for x in m2])
    outer = _mm(outer, [eye + x for x in m4])
    t0 = _mm(outer, dinv)
    nt = [_dot(n.astype(BF16), jnp.concatenate(_split(t), axis=1)) for n, t in zip(ns, t0)]
    res = [(eye - t) - (p[:, :c] + p[:, c:]) for t, p in zip(t0, nt)]
    return [t + r for t, r in zip(t0, _mm(t0, res))]


def _gdn_core_kernel(q_ref, k_ref, v_ref, gb_ref, gate_ref, x_ref, head_ref, og_ref, wo_ref,
                     out_ref, s_ref, o_ref):
    chunk = pl.program_id(1)

    @pl.when(chunk == 0)
    def _():
        s_ref[...] = jnp.zeros_like(s_ref)

    c = GCHUNK
    ri = lax.broadcasted_iota(jnp.int32, (c, c), 0)
    ci = lax.broadcasted_iota(jnp.int32, (c, c), 1)
    incl = ri >= ci
    strict = ri > ci
    blk_same = (ri // 16) == (ci // 16)
    tril = incl.astype(BF16)
    scale = GDN_HEAD_DIM ** -0.5
    heads = range(GDN_HEADS)
    sls = [slice(hd * GDN_HEAD_DIM, (hd + 1) * GDN_HEAD_DIM) for hd in heads]
    ks = [k_ref[:, sl] for sl in sls]
    kbs = [k.astype(BF16) for k in ks]
    bmats = [jnp.broadcast_to(gb_ref[:, GDN_HEADS + hd:GDN_HEADS + hd + 1], (c, c)) for hd in heads]
    gcols = [_dot_exact_lhs(tril, jnp.broadcast_to(gb_ref[:, hd:hd + 1], (c, c))) for hd in heads]
    kks = [_dot_nt(kb, kb) for kb in kbs]
    decays = [jnp.where(incl, jnp.exp(jnp.where(incl, g - g.T, 0.0)), 0.0) for g in gcols]
    ns = [jnp.where(strict, b * d * kk, 0.0) for b, d, kk in zip(bmats, decays, kks)]
    ts = _unit_lower_inverses(ns, blk_same)
    egs = [jnp.exp(g) for g in gcols]
    rhs = [jnp.concatenate([b * v_ref[:, sl], (b * eg) * k], axis=1).astype(BF16)
           for b, eg, k, sl in zip(bmats, egs, ks, sls)]
    sols = [_dot(t.astype(BF16), r) for t, r in zip(ts, rhs)]
    qs = [q_ref[:, sl] * scale for sl in sls]
    ats = [_dot_nt(q.astype(BF16), kb) * d for q, kb, d in zip(qs, kbs, decays)]
    glasts = [g[c - 1:c, :] for g in gcols]
    wcs = [s[:, GDN_HEAD_DIM:].astype(BF16) for s in sols]
    qds = [(q * eg).astype(BF16) for q, eg in zip(qs, egs)]
    kdts = [(k * jnp.exp(gl - g)).T.astype(BF16) for k, gl, g in zip(ks, glasts, gcols)]
    atbs = [a.astype(BF16) for a in ats]

    ss = [s_ref[hd] for hd in heads]
    sbs = [s.astype(BF16) for s in ss]
    ps = [_dot(jnp.concatenate([wc, qd], axis=0), sb) for wc, qd, sb in zip(wcs, qds, sbs)]
    ubs = [(sol[:, :GDN_HEAD_DIM] - p[:c]).astype(BF16) for sol, p in zip(sols, ps)]
    rs = [_dot(jnp.concatenate([at, kdt], axis=0), ub) for at, kdt, ub in zip(atbs, kdts, ubs)]
    for hd in heads:
        s_ref[hd] = ss[hd] * jnp.exp(glasts[hd]) + rs[hd][c:]
    for hd in heads:
        o = ps[hd][c:] + rs[hd][:c]
        o = o * lax.rsqrt(jnp.mean(o * o, axis=-1, keepdims=True) + EPS) * og_ref[...]
        gate = gate_ref[:, sls[hd]]
        o_ref[:, sls[hd]] = (o * (gate * _sigmoid(gate))).astype(BF16)
    resid = jnp.where(chunk == 0, head_ref[...], x_ref[0])
    out_ref[...] = resid + _dot(o_ref[...], wo_ref[...])


def _gdn_core(q, k, v, gb, gate, x, head, og, wo):
    m = q.shape[0]
    batch = x.shape[0]
    nch = m // GCHUNK // batch
    row = lambda b, c: (b * nch + c, 0)
    const = lambda b, c: (0, 0)
    wide = pl.BlockSpec((GCHUNK, GDN_WIDTH), row)
    return pl.pallas_call(
        _gdn_core_kernel,
        grid=(batch, nch),
        in_specs=[wide, wide, wide,
                  pl.BlockSpec((GCHUNK, LANES), row),
                  wide,
                  pl.BlockSpec((1, GCHUNK, D_MODEL), lambda b, c: (b, jnp.maximum(c - 1, 0), 0)),
                  pl.BlockSpec((GCHUNK, D_MODEL), const),
                  pl.BlockSpec((1, GDN_HEAD_DIM), const),
                  pl.BlockSpec((GDN_WIDTH, D_MODEL), const)],
        out_specs=pl.BlockSpec((GCHUNK, D_MODEL), row),
        out_shape=jax.ShapeDtypeStruct((m, D_MODEL), F32),
        scratch_shapes=[pltpu.VMEM((GDN_HEADS, GDN_HEAD_DIM, GDN_HEAD_DIM), F32),
                        pltpu.VMEM((GCHUNK, GDN_WIDTH), BF16)],
        compiler_params=_params(("arbitrary", "arbitrary")),
        name="gdn_core",
    )(q, k, v, gb, gate, x, head, og, wo)


def _ffn_chunks():
    step = 3 * MXU_DIM
    return [(lo, min(lo + step, D_FF)) for lo in range(0, D_FF, step)]


def _ffn_body(x, g_ref, wgu_ref, wd_ref):
    xn = _rms(x, g_ref[...]).astype(BF16)

    def gate_up(lo, hi):
        return _dot(xn, wgu_ref[:, lo:hi]), _dot(xn, wgu_ref[:, D_FF + lo:D_FF + hi])

    chunks = _ffn_chunks()
    acc = x
    pending = gate_up(*chunks[0])
    for i, (lo, hi) in enumerate(chunks):
        gate, up = pending
        if i + 1 < len(chunks):
            pending = gate_up(*chunks[i + 1])
        act = (gate * _sigmoid(gate) * up).astype(BF16)
        acc = acc + _dot(act, wd_ref[lo:hi, :])
    return acc


def _ffn_kernel(x_ref, g_ref, wgu_ref, wd_ref, out_ref):
    out_ref[...] = _ffn_body(x_ref[...], g_ref, wgu_ref, wd_ref)


def _ffn_final_kernel(x_ref, g_ref, wgu_ref, wd_ref, gf_ref, out_ref):
    out_ref[0] = _rms(_ffn_body(x_ref[...], g_ref, wgu_ref, wd_ref), gf_ref[...])


def _ffn_weight_specs(const):
    return [pl.BlockSpec((1, D_MODEL), const),
            pl.BlockSpec((D_MODEL, 2 * D_FF), const, pipeline_mode=pl.Buffered(1)),
            pl.BlockSpec((D_FF, D_MODEL), const, pipeline_mode=pl.Buffered(1))]


def _ffn(x, g, w_gu, w_d, *, tm):
    m = x.shape[0]
    row = lambda r: (r, 0)
    return pl.pallas_call(
        _ffn_kernel,
        grid=(m // tm,),
        in_specs=[pl.BlockSpec((tm, D_MODEL), row)] + _ffn_weight_specs(lambda r: (0, 0)),
        out_specs=pl.BlockSpec((tm, D_MODEL), row),
        out_shape=jax.ShapeDtypeStruct((m, D_MODEL), F32),
        compiler_params=_params(("parallel",)),
        name="ffn",
    )(x, g, w_gu, w_d)


def _ffn_final(x, g, w_gu, w_d, gf, *, batch, lp, seq):
    rows = max(r for r in (512, 256, 128) if seq % r == 0)
    const = lambda b, r: (0, 0)
    return pl.pallas_call(
        _ffn_final_kernel,
        grid=(batch, seq // rows),
        in_specs=[pl.BlockSpec((pl.Element(rows), pl.Element(D_MODEL)),
                               lambda b, r: (pl.multiple_of(b * lp + GCHUNK + r * rows, GCHUNK), 0))]
        + _ffn_weight_specs(const) + [pl.BlockSpec((1, D_MODEL), const)],
        out_specs=pl.BlockSpec((1, rows, D_MODEL), lambda b, r: (b, r, 0)),
        out_shape=jax.ShapeDtypeStruct((batch, seq, D_MODEL), F32),
        compiler_params=_params(("parallel", "parallel")),
        name="ffn_final",
    )(x, g, w_gu, w_d, gf)


def _qkv_kernel(x_ref, gkv_ref, gq_ref, wkv_ref, wq_ref, q_ref, k_ref, v_ref):
    x = x_ref[...]
    inv = lax.rsqrt(jnp.mean(x * x, axis=-1, keepdims=True) + EPS)
    xkv = (x * inv * gkv_ref[...]).astype(BF16)
    xq = (x * inv * gq_ref[...]).astype(BF16)
    k_ref[...] = _dot(xkv, wkv_ref[:, :SB_WIDTH]).astype(BF16)
    v_ref[...] = _dot(xkv, wkv_ref[:, SB_WIDTH:]).astype(BF16)
    q_ref[...] = (_dot(xq, wq_ref[...]) * (SB_HEAD_DIM ** -0.5)).astype(BF16)


def _qkv(x, gkv, gq, wkv, wq, *, tm):
    m = x.shape[0]
    row = lambda r: (r, 0)
    const = lambda r: (0, 0)
    out = pl.BlockSpec((tm, SB_WIDTH), row)
    return pl.pallas_call(
        _qkv_kernel,
        grid=(m // tm,),
        in_specs=[pl.BlockSpec((tm, D_MODEL), row),
                  pl.BlockSpec((1, D_MODEL), const),
                  pl.BlockSpec((1, D_MODEL), const),
                  pl.BlockSpec((D_MODEL, 2 * SB_WIDTH), const),
                  pl.BlockSpec((D_MODEL, SB_WIDTH), const)],
        out_specs=[out, out, out],
        out_shape=[jax.ShapeDtypeStruct((m, SB_WIDTH), BF16)] * 3,
        compiler_params=_params(("parallel",)),
        name="sb_qkv",
    )(x, gkv, gq, wkv, wq)


def _sb_kernel(q_ref, k_ref, v_ref, h_ref, wo_ref, out_ref, c_ref, o_ref, obf_ref, *, group):
    tq = SB_BLOCK
    iq0 = pl.program_id(1) * group
    first = lax.broadcasted_iota(jnp.int32, (tq, LANES), 1) < SB_HEAD_DIM
    ri = lax.broadcasted_iota(jnp.int32, (tq, tq), 0)
    ci = lax.broadcasted_iota(jnp.int32, (tq, tq), 1)
    upper_ext = jnp.concatenate([(ri > ci).astype(BF16), jnp.ones((tq, LANES), BF16)], axis=1)
    pairs = range(SB_PAIRS)
    cols = [slice(p * LANES, (p + 1) * LANES) for p in pairs]

    def stacked_q(s, p):
        q2 = q_ref[s * tq:(s + 1) * tq, cols[p]]
        zero = jnp.zeros_like(q2)
        return jnp.concatenate([jnp.where(first, q2, zero), jnp.where(first, zero, q2)], axis=0)

    c_ref[...] = jnp.zeros_like(c_ref)
    o_ref[...] = jnp.zeros_like(o_ref)

    def visit(s, j, bias):
        start = pl.multiple_of(j * tq, tq)
        slot = [s * SB_PAIRS + p for p in pairs]
        zs = [_dot_nt(stacked_q(s, p), k_ref[pl.ds(start, tq), cols[p]]) + bias for p in pairs]
        sps = [_softplus(z) for z in zs]
        accs = [_dot(sp.astype(BF16), upper_ext) for sp in sps]
        ws = []
        cmin = None
        for p in pairs:
            c = c_ref[slot[p]]
            ws.append(jnp.exp(zs[p] - sps[p] - accs[p][:, :tq] - c).astype(BF16))
            c = c + accs[p][:, tq:]
            c_ref[slot[p]] = c
            cmin = c if cmin is None else jnp.minimum(cmin, c)
        for p in pairs:
            vj = v_ref[pl.ds(start, tq), cols[p]]
            zero = jnp.zeros_like(vj)
            v2 = jnp.concatenate([jnp.where(first, vj, zero), jnp.where(first, zero, vj)], axis=0)
            o_ref[slot[p]] += _dot(jnp.concatenate([ws[p][:tq], ws[p][tq:]], axis=1), v2)
        return jnp.min(cmin)

    def row_bias(j, live):
        kp = j * tq + lax.broadcasted_iota(jnp.int32, (1, tq), 1)
        return jnp.where(jnp.logical_and(kp >= FRONT, live), 0.0, SB_MASKED)

    for s in range(group):
        kpos = (iq0 + s) * tq + ci
        diag = jnp.where(jnp.logical_and(ci < ri, kpos >= FRONT), 0.0, SB_MASKED)
        visit(s, iq0 + s, jnp.concatenate([diag, diag], axis=0))

    lowest = []
    for s in range(group):
        prev = jnp.maximum(iq0 + s - 1, 0)
        lowest.append(visit(s, prev, row_bias(prev, iq0 + s >= 1)))

    def cond(carry):
        j, low = carry
        return jnp.logical_and(j >= 0, low < SB_SKIP)

    for s in range(group):
        def body(carry, s=s):
            j, _ = carry
            return j - 1, visit(s, j, row_bias(j, True))

        lax.while_loop(cond, body, (iq0 + s - 2, lowest[s]))

    for s in range(group):
        for p in pairs:
            obf_ref[s * tq:(s + 1) * tq, cols[p]] = o_ref[s * SB_PAIRS + p].astype(BF16)
    out_ref[...] = h_ref[...] + _dot(obf_ref[...], wo_ref[...])


def _sb_layer(h, q, k, v, wo, *, batch, lp):
    m = q.shape[0]
    nq = lp // SB_BLOCK
    tq = SB_BLOCK
    group = max(g for g in (5, 4, 3, 2, 1) if nq % g == 0)
    steps = nq // group
    row = lambda b, i: (b * steps + i, 0)
    whole = lambda b, i: (b, 0)
    return pl.pallas_call(
        functools.partial(_sb_kernel, group=group),
        grid=(batch, steps),
        in_specs=[pl.BlockSpec((group * tq, SB_WIDTH), row),
                  pl.BlockSpec((lp, SB_WIDTH), whole, pipeline_mode=pl.Buffered(1)),
                  pl.BlockSpec((lp, SB_WIDTH), whole, pipeline_mode=pl.Buffered(1)),
                  pl.BlockSpec((group * tq, D_MODEL), row),
                  pl.BlockSpec((SB_WIDTH, D_MODEL), lambda b, i: (0, 0), pipeline_mode=pl.Buffered(1))],
        out_specs=pl.BlockSpec((group * tq, D_MODEL), row, pipeline_mode=pl.Buffered(1)),
        out_shape=jax.ShapeDtypeStruct((m, D_MODEL), F32),
        scratch_shapes=[pltpu.VMEM((group * SB_PAIRS, 2 * tq, tq), F32),
                        pltpu.VMEM((group * SB_PAIRS, tq, LANES), F32),
                        pltpu.VMEM((group * tq, SB_WIDTH), BF16)],
        compiler_params=_params(("parallel", "arbitrary")),
        name="sb_attn",
    )(q, k, v, h, wo)


def _row_tile(lp):
    for tm in (640, 512, 384, 256, 128):
        if lp % tm == 0:
            return tm
    raise ValueError(f"padded length {lp} is not a multiple of {GCHUNK}")


def kernel(x, meta_tokens, gdn_norm_g, gdn_w_in, gdn_conv_w, gdn_a_log, gdn_dt_bias, gdn_onorm_g, gdn_w_out,
           kv_norm_g, w_kv, sb_norm_g, sb_w_q, sb_w_o, ffn_norm_g, ffn_w_gate_up, ffn_w_down, final_norm_g):
    batch, seq, _ = x.shape
    assert seq % GCHUNK == 0 and gdn_w_in.shape[0] == 1 and sb_w_q.shape[0] == 1
    lp = GCHUNK + seq
    tm = _row_tile(lp)

    assert tm > GCHUNK and seq >= tm
    head = jnp.concatenate([jnp.zeros((FRONT, D_MODEL), x.dtype), meta_tokens.astype(x.dtype)], axis=0)

    w_in = gdn_w_in[0]
    w_main = w_in[:, :4 * GDN_WIDTH].astype(BF16)
    w_ab = jnp.pad(w_in[:, 4 * GDN_WIDTH:], ((0, 0), (0, LANES - 2 * GDN_HEADS))).astype(BF16)
    ab_par = jnp.zeros((SUBLANES, LANES), F32)
    ab_par = ab_par.at[0, :GDN_HEADS].set(gdn_a_log[0]).at[1, :GDN_HEADS].set(gdn_dt_bias[0])

    q, k, v, gate, gb = _gdn_in(x, head, gdn_norm_g[0][None], w_main, w_ab, gdn_conv_w[0], ab_par, tm=tm, lp=lp)
    h = _gdn_core(q, k, v, gb, gate, x, head, gdn_onorm_g[0][None], gdn_w_out[0].astype(BF16))
    h = _ffn(h, ffn_norm_g[0][None], ffn_w_gate_up[0].astype(BF16), ffn_w_down[0].astype(BF16), tm=tm)

    sq, sk, sv = _qkv(h, kv_norm_g[None], sb_norm_g[0][None], w_kv.astype(BF16), sb_w_q[0].astype(BF16), tm=tm)
    h = _sb_layer(h, sq, sk, sv, sb_w_o[0].astype(BF16), batch=batch, lp=lp)
    return _ffn_final(h, ffn_norm_g[1][None], ffn_w_gate_up[1].astype(BF16), ffn_w_down[1].astype(BF16),
                      final_norm_g[None], batch=batch, lp=lp, seq=seq)
```

```python
import functools

import jax
import jax.numpy as jnp
from jax import lax
from jax.experimental import pallas as pl
from jax.experimental.pallas import tpu as pltpu

F32 = jnp.float32
BF16 = jnp.bfloat16

D_MODEL = 1024
N_META = 16
GDN_HEADS = 8
GDN_HEAD_DIM = 128
GDN_WIDTH = GDN_HEADS * GDN_HEAD_DIM
CONV_WIDTH = 4
SB_HEADS = 16
SB_HEAD_DIM = 64
SB_WIDTH = SB_HEADS * SB_HEAD_DIM
D_FF = 2816
EPS = 1e-6

LANES = 128
SUBLANES = 8
MXU_DIM = 256
NEG_LOG2E = -1.4426950408889634
GCHUNK = 128
FRONT = GCHUNK - N_META
SB_BLOCK = 128
SB_PAIRS = SB_WIDTH // LANES
SB_SKIP = 88.0
SB_TOP = 32
SB_MASKED = -1e30
VMEM_LIMIT = 60 * 1024 * 1024


def _dot(a, b):
    return lax.dot_general(a, b, (((1,), (0,)), ((), ())), preferred_element_type=F32)


def _dot_nt(a, b):
    return lax.dot_general(a, b, (((1,), (1,)), ((), ())), preferred_element_type=F32)


def _split(a):
    hi = a.astype(BF16)
    lo = (a - hi.astype(F32)).astype(BF16)
    return hi, lo


def _dot_exact_lhs(m_bf16, b):
    bh, bl = _split(b)
    return _dot(m_bf16, bh) + _dot(m_bf16, bl)


def _rms(x, g):
    return x * lax.rsqrt(jnp.mean(x * x, axis=-1, keepdims=True) + EPS) * g


def _sigmoid(x):
    return 1.0 / (1.0 + jnp.exp2(x * NEG_LOG2E))


def _softplus(x):
    return jnp.maximum(x, 0.0) + jnp.log(1.0 + jnp.exp(-jnp.abs(x)))


def _params(sem):
    return pltpu.CompilerParams(dimension_semantics=sem, vmem_limit_bytes=VMEM_LIMIT)


def _gdn_in_kernel(x_ref, head_ref, ng_ref, w_ref, wab_ref, cw_ref, ab_ref,
                   q_ref, k_ref, v_ref, gate_ref, gb_ref,
                   carry_ref, *, tm):
    tb = pl.program_id(1)

    @pl.when(tb == 0)
    def _():
        carry_ref[...] = jnp.zeros_like(carry_ref)

    xt = x_ref[0]
    h = jnp.where(tb == 0, jnp.concatenate([head_ref[...], xt[:tm - GCHUNK]], axis=0), xt)
    xn = _rms(h, ng_ref[...]).astype(BF16)
    outs = (q_ref, k_ref, v_ref)

    def project(part):
        return _dot(xn, w_ref[:, part * GDN_WIDTH:(part + 1) * GDN_WIDTH])

    pending = project(0)
    for part in range(3):
        cols = slice(part * GDN_WIDTH, (part + 1) * GDN_WIDTH)
        p = pending
        pending = project(part + 1)
        ext = jnp.concatenate([carry_ref[part], p], axis=0)
        carry_ref[part] = p[tm - SUBLANES:, :]
        w0, w1, w2, w3 = (cw_ref[i:i + 1, cols] for i in range(CONV_WIDTH))
        prev = pltpu.roll(ext, 1, axis=0)
        u = ext * w1 + prev * w0
        y = p * w3 + prev[SUBLANES:] * w2 + pltpu.roll(u, 2, axis=0)[SUBLANES:]
        y = y * _sigmoid(y)
        if part < 2:
            for hd in range(GDN_HEADS):
                sl = slice(hd * GDN_HEAD_DIM, (hd + 1) * GDN_HEAD_DIM)
                seg = y[:, sl]
                outs[part][:, sl] = seg * lax.rsqrt(jnp.sum(seg * seg, axis=-1, keepdims=True) + EPS)
        else:
            v_ref[...] = y
    gate_ref[...] = pending

    ab = _dot(xn, wab_ref[...])
    a_log = ab_ref[0:1, :]
    dt_bias = ab_ref[1:2, :]
    g = -jnp.exp(a_log) * _softplus(ab + dt_bias)
    beta = _sigmoid(ab)
    lane = lax.broadcasted_iota(jnp.int32, ab.shape, 1)
    pos = tb * tm + lax.broadcasted_iota(jnp.int32, ab.shape, 0)
    gb_ref[...] = jnp.where(pos >= FRONT, jnp.where(lane < GDN_HEADS, g, beta), 0.0)


def _gdn_in(x, head, ng, w_main, w_ab, conv_w, ab_par, *, tm, lp):
    batch = x.shape[0]
    m = batch * lp
    tiles = lp // tm
    row = lambda b, t: (b * tiles + t, 0)
    const = lambda b, t: (0, 0)
    wide = pl.BlockSpec((tm, GDN_WIDTH), row)
    return pl.pallas_call(
        functools.partial(_gdn_in_kernel, tm=tm),
        grid=(batch, tiles),
        in_specs=[
            pl.BlockSpec((pl.Element(1), pl.Element(tm), pl.Element(D_MODEL)),
                         lambda b, t: (b, pl.multiple_of(jnp.maximum(t * tm - GCHUNK, 0), GCHUNK), 0)),
            pl.BlockSpec((GCHUNK, D_MODEL), const),
            pl.BlockSpec((1, D_MODEL), const),
            pl.BlockSpec((D_MODEL, 4 * GDN_WIDTH), const),
            pl.BlockSpec((D_MODEL, LANES), const),
            pl.BlockSpec((CONV_WIDTH, 3 * GDN_WIDTH), const),
            pl.BlockSpec((SUBLANES, LANES), const),
        ],
        out_specs=[wide, wide, wide, wide, pl.BlockSpec((tm, LANES), row)],
        out_shape=[jax.ShapeDtypeStruct((m, GDN_WIDTH), F32)] * 4 + [jax.ShapeDtypeStruct((m, LANES), F32)],
        scratch_shapes=[pltpu.VMEM((3, SUBLANES, GDN_WIDTH), F32)],
        compiler_params=_params(("arbitrary", "arbitrary")),
        name="gdn_in",
    )(x, head, ng, w_main, w_ab, conv_w, ab_par)


def _mm(xs, ys):
    return [_dot(x.astype(BF16), y.astype(BF16)) for x, y in zip(xs, ys)]


def _unit_lower_inverses(ns, blk_same):
    c = ns[0].shape[0]
    eye = (lax.broadcasted_iota(jnp.int32, (c, c), 0) == lax.broadcasted_iota(jnp.int32, (c, c), 1)).astype(F32)
    nd = [jnp.where(blk_same, n, 0.0) for n in ns]
    no = [n - d for n, d in zip(ns, nd)]
    nd2 = _mm(nd, nd)
    nd4 = _mm(nd2, nd2)
    nd8 = _mm(nd4, nd4)
    dinv = _mm([eye - x for x in nd], [eye + x for x in nd2])
    dinv = _mm(dinv, [eye + x for x in nd4])
    dinv = _mm(dinv, [eye + x for x in nd8])
    m1 = _mm(dinv, no)
    m2 = _mm(m1, m1)
    m4 = _mm(m2, m2)
    outer = _mm([eye - x for x in m1], [eye + x for x in m2])
    outer = _mm(outer, [eye + x for x in m4])
    t0 = _mm(outer, dinv)
    nt = [_dot(n.astype(BF16), jnp.concatenate(_split(t), axis=1)) for n, t in zip(ns, t0)]
    res = [(eye - t) - (p[:, :c] + p[:, c:]) for t, p in zip(t0, nt)]
    return [t + r for t, r in zip(t0, _mm(t0, res))]


def _gdn_core_kernel(q_ref, k_ref, v_ref, gb_ref, gate_ref, x_ref, head_ref, og_ref, wo_ref,
                     out_ref, s_ref, o_ref):
    chunk = pl.program_id(1)

    @pl.when(chunk == 0)
    def _():
        s_ref[...] = jnp.zeros_like(s_ref)

    c = GCHUNK
    ri = lax.broadcasted_iota(jnp.int32, (c, c), 0)
    ci = lax.broadcasted_iota(jnp.int32, (c, c), 1)
    incl = ri >= ci
    strict = ri > ci
    blk_same = (ri // 16) == (ci // 16)
    tril = incl.astype(BF16)
    scale = GDN_HEAD_DIM ** -0.5
    heads = range(GDN_HEADS)
    sls = [slice(hd * GDN_HEAD_DIM, (hd + 1) * GDN_HEAD_DIM) for hd in heads]
    ks = [k_ref[:, sl] for sl in sls]
    kbs = [k.astype(BF16) for k in ks]
    bmats = [jnp.broadcast_to(gb_ref[:, GDN_HEADS + hd:GDN_HEADS + hd + 1], (c, c)) for hd in heads]
    gcols = [_dot_exact_lhs(tril, jnp.broadcast_to(gb_ref[:, hd:hd + 1], (c, c))) for hd in heads]
    kks = [_dot_nt(kb, kb) for kb in kbs]
    decays = [jnp.where(incl, jnp.exp(jnp.where(incl, g - g.T, 0.0)), 0.0) for g in gcols]
    ns = [jnp.where(strict, b * d * kk, 0.0) for b, d, kk in zip(bmats, decays, kks)]
    ts = _unit_lower_inverses(ns, blk_same)
    egs = [jnp.exp(g) for g in gcols]
    rhs = [jnp.concatenate([b * v_ref[:, sl], (b * eg) * k], axis=1).astype(BF16)
           for b, eg, k, sl in zip(bmats, egs, ks, sls)]
    sols = [_dot(t.astype(BF16), r) for t, r in zip(ts, rhs)]
    qs = [q_ref[:, sl] * scale for sl in sls]
    ats = [_dot_nt(q.astype(BF16), kb) * d for q, kb, d in zip(qs, kbs, decays)]
    glasts = [g[c - 1:c, :] for g in gcols]
    wcs = [s[:, GDN_HEAD_DIM:].astype(BF16) for s in sols]
    qds = [(q * eg).astype(BF16) for q, eg in zip(qs, egs)]
    kdts = [(k * jnp.exp(gl - g)).T.astype(BF16) for k, gl, g in zip(ks, glasts, gcols)]
    atbs = [a.astype(BF16) for a in ats]

    ss = [s_ref[hd] for hd in heads]
    sbs = [s.astype(BF16) for s in ss]
    ps = [_dot(jnp.concatenate([wc, qd], axis=0), sb) for wc, qd, sb in zip(wcs, qds, sbs)]
    ubs = [(sol[:, :GDN_HEAD_DIM] - p[:c]).astype(BF16) for sol, p in zip(sols, ps)]
    rs = [_dot(jnp.concatenate([at, kdt], axis=0), ub) for at, kdt, ub in zip(atbs, kdts, ubs)]
    for hd in heads:
        s_ref[hd] = ss[hd] * jnp.exp(glasts[hd]) + rs[hd][c:]
    for hd in heads:
        o = ps[hd][c:] + rs[hd][:c]
        o = o * lax.rsqrt(jnp.mean(o * o, axis=-1, keepdims=True) + EPS) * og_ref[...]
        gate = gate_ref[:, sls[hd]]
        o_ref[:, sls[hd]] = (o * (gate * _sigmoid(gate))).astype(BF16)
    resid = jnp.where(chunk == 0, head_ref[...], x_ref[0])
    out_ref[...] = resid + _dot(o_ref[...], wo_ref[...])


def _gdn_core(q, k, v, gb, gate, x, head, og, wo):
    m = q.shape[0]
    batch = x.shape[0]
    nch = m // GCHUNK // batch
    row = lambda b, c: (b * nch + c, 0)
    const = lambda b, c: (0, 0)
    wide = pl.BlockSpec((GCHUNK, GDN_WIDTH), row)
    return pl.pallas_call(
        _gdn_core_kernel,
        grid=(batch, nch),
        in_specs=[wide, wide, wide,
                  pl.BlockSpec((GCHUNK, LANES), row),
                  wide,
                  pl.BlockSpec((1, GCHUNK, D_MODEL), lambda b, c: (b, jnp.maximum(c - 1, 0), 0)),
                  pl.BlockSpec((GCHUNK, D_MODEL), const),
                  pl.BlockSpec((1, GDN_HEAD_DIM), const),
                  pl.BlockSpec((GDN_WIDTH, D_MODEL), const)],
        out_specs=pl.BlockSpec((GCHUNK, D_MODEL), row),
        out_shape=jax.ShapeDtypeStruct((m, D_MODEL), F32),
        scratch_shapes=[pltpu.VMEM((GDN_HEADS, GDN_HEAD_DIM, GDN_HEAD_DIM), F32),
                        pltpu.VMEM((GCHUNK, GDN_WIDTH), BF16)],
        compiler_params=_params(("arbitrary", "arbitrary")),
        name="gdn_core",
    )(q, k, v, gb, gate, x, head, og, wo)


def _ffn_chunks():
    step = 3 * MXU_DIM
    return [(lo, min(lo + step, D_FF)) for lo in range(0, D_FF, step)]


def _ffn_body(x, g_ref, wgu_ref, wd_ref):
    xn = _rms(x, g_ref[...]).astype(BF16)

    def gate_up(lo, hi):
        return _dot(xn, wgu_ref[:, lo:hi]), _dot(xn, wgu_ref[:, D_FF + lo:D_FF + hi])

    chunks = _ffn_chunks()
    acc = x
    pending = gate_up(*chunks[0])
    for i, (lo, hi) in enumerate(chunks):
        gate, up = pending
        if i + 1 < len(chunks):
            pending = gate_up(*chunks[i + 1])
        act = (gate * _sigmoid(gate) * up).astype(BF16)
        acc = acc + _dot(act, wd_ref[lo:hi, :])
    return acc


def _ffn_kernel(x_ref, g_ref, wgu_ref, wd_ref, out_ref):
    out_ref[...] = _ffn_body(x_ref[...], g_ref, wgu_ref, wd_ref)


def _ffn_final_kernel(x_ref, g_ref, wgu_ref, wd_ref, gf_ref, out_ref):
    out_ref[0] = _rms(_ffn_body(x_ref[...], g_ref, wgu_ref, wd_ref), gf_ref[...])


def _ffn_weight_specs(const):
    return [pl.BlockSpec((1, D_MODEL), const),
            pl.BlockSpec((D_MODEL, 2 * D_FF), const, pipeline_mode=pl.Buffered(1)),
            pl.BlockSpec((D_FF, D_MODEL), const, pipeline_mode=pl.Buffered(1))]


def _ffn(x, g, w_gu, w_d, *, tm):
    m = x.shape[0]
    row = lambda r: (r, 0)
    return pl.pallas_call(
        _ffn_kernel,
        grid=(m // tm,),
        in_specs=[pl.BlockSpec((tm, D_MODEL), row)] + _ffn_weight_specs(lambda r: (0, 0)),
        out_specs=pl.BlockSpec((tm, D_MODEL), row),
        out_shape=jax.ShapeDtypeStruct((m, D_MODEL), F32),
        compiler_params=_params(("parallel",)),
        name="ffn",
    )(x, g, w_gu, w_d)


def _ffn_final(x, g, w_gu, w_d, gf, *, batch, lp, seq):
    rows = max(r for r in (512, 256, 128) if seq % r == 0)
    const = lambda b, r: (0, 0)
    return pl.pallas_call(
        _ffn_final_kernel,
        grid=(batch, seq // rows),
        in_specs=[pl.BlockSpec((pl.Element(rows), pl.Element(D_MODEL)),
                               lambda b, r: (pl.multiple_of(b * lp + GCHUNK + r * rows, GCHUNK), 0))]
        + _ffn_weight_specs(const) + [pl.BlockSpec((1, D_MODEL), const)],
        out_specs=pl.BlockSpec((1, rows, D_MODEL), lambda b, r: (b, r, 0)),
        out_shape=jax.ShapeDtypeStruct((batch, seq, D_MODEL), F32),
        compiler_params=_params(("parallel", "parallel")),
        name="ffn_final",
    )(x, g, w_gu, w_d, gf)


def _qkv_kernel(x_ref, gkv_ref, gq_ref, wkv_ref, wq_ref, q_ref, k_ref, v_ref):
    x = x_ref[...]
    inv = lax.rsqrt(jnp.mean(x * x, axis=-1, keepdims=True) + EPS)
    xkv = (x * inv * gkv_ref[...]).astype(BF16)
    xq = (x * inv * gq_ref[...]).astype(BF16)
    k_ref[...] = _dot(xkv, wkv_ref[:, :SB_WIDTH]).astype(BF16)
    v_ref[...] = _dot(xkv, wkv_ref[:, SB_WIDTH:]).astype(BF16)
    q_ref[...] = (_dot(xq, wq_ref[...]) * (SB_HEAD_DIM ** -0.5)).astype(BF16)


def _qkv(x, gkv, gq, wkv, wq, *, tm):
    m = x.shape[0]
    row = lambda r: (r, 0)
    const = lambda r: (0, 0)
    out = pl.BlockSpec((tm, SB_WIDTH), row)
    return pl.pallas_call(
        _qkv_kernel,
        grid=(m // tm,),
        in_specs=[pl.BlockSpec((tm, D_MODEL), row),
                  pl.BlockSpec((1, D_MODEL), const),
                  pl.BlockSpec((1, D_MODEL), const),
                  pl.BlockSpec((D_MODEL, 2 * SB_WIDTH), const),
                  pl.BlockSpec((D_MODEL, SB_WIDTH), const)],
        out_specs=[out, out, out],
        out_shape=[jax.ShapeDtypeStruct((m, SB_WIDTH), BF16)] * 3,
        compiler_params=_params(("parallel",)),
        name="sb_qkv",
    )(x, gkv, gq, wkv, wq)


def _sb_kernel(q_ref, k_ref, v_ref, h_ref, wo_ref, out_ref, c_ref, o_ref, obf_ref, *, group):
    tq = SB_BLOCK
    iq0 = pl.program_id(1) * group
    first = lax.broadcasted_iota(jnp.int32, (tq, LANES), 1) < SB_HEAD_DIM
    ri = lax.broadcasted_iota(jnp.int32, (tq, tq), 0)
    ci = lax.broadcasted_iota(jnp.int32, (tq, tq), 1)
    upper_ext = jnp.concatenate([(ri > ci).astype(BF16), jnp.ones((tq, LANES), BF16)], axis=1)
    pairs = range(SB_PAIRS)
    cols = [slice(p * LANES, (p + 1) * LANES) for p in pairs]

    c_ref[...] = jnp.zeros_like(c_ref)
    o_ref[...] = jnp.zeros_like(o_ref)

    def visit(s, j, bias, lo=0, hi=tq):
        start = pl.multiple_of(j * tq, tq)
        slot = [s * SB_PAIRS + p for p in pairs]
        rows = hi - lo
        first_q = lax.broadcasted_iota(jnp.int32, (rows, LANES), 1) < SB_HEAD_DIM

        def stacked_q(p):
            q2 = q_ref[s * tq + lo:s * tq + hi, cols[p]]
            zero = jnp.zeros_like(q2)
            return jnp.concatenate([jnp.where(first_q, q2, zero), jnp.where(first_q, zero, q2)], axis=0)

        zs = [_dot_nt(stacked_q(p), k_ref[pl.ds(start, tq), cols[p]]) + bias for p in pairs]
        sps = [_softplus(z) for z in zs]
        accs = [_dot(sp.astype(BF16), upper_ext) for sp in sps]
        ws = []
        cmin = None
        for p in pairs:
            c = jnp.concatenate([c_ref[slot[p], lo:hi, :], c_ref[slot[p], tq + lo:tq + hi, :]], axis=0)
            ws.append(jnp.exp(zs[p] - sps[p] - accs[p][:, :tq] - c).astype(BF16))
            c = c + accs[p][:, tq:]
            c_ref[slot[p], lo:hi, :] = c[:rows]
            c_ref[slot[p], tq + lo:tq + hi, :] = c[rows:]
            cmin = c if cmin is None else jnp.minimum(cmin, c)
        for p in pairs:
            vj = v_ref[pl.ds(start, tq), cols[p]]
            zero = jnp.zeros_like(vj)
            v2 = jnp.concatenate([jnp.where(first, vj, zero), jnp.where(first, zero, vj)], axis=0)
            o_ref[slot[p], lo:hi, :] += _dot(jnp.concatenate([ws[p][:rows], ws[p][rows:]], axis=1), v2)
        return cmin

    def row_bias(j, live):
        kp = j * tq + lax.broadcasted_iota(jnp.int32, (1, tq), 1)
        return jnp.where(jnp.logical_and(kp >= FRONT, live), 0.0, SB_MASKED)

    for s in range(group):
        kpos = (iq0 + s) * tq + ci
        diag = jnp.where(jnp.logical_and(ci < ri, kpos >= FRONT), 0.0, SB_MASKED)
        visit(s, iq0 + s, jnp.concatenate([diag, diag], axis=0))

    rests = []
    for s in range(group):
        prev = jnp.maximum(iq0 + s - 1, 0)
        cmin = visit(s, prev, row_bias(prev, iq0 + s >= 1))
        rests.append(jnp.min(jnp.minimum(cmin[SB_TOP:tq], cmin[tq + SB_TOP:])))

    tops = []
    for s in range(group):
        prev = jnp.maximum(iq0 + s - 2, 0)
        tops.append(jnp.min(visit(s, prev, row_bias(prev, iq0 + s >= 2), hi=SB_TOP)))

    def cond(carry):
        j, low = carry
        return jnp.logical_and(j >= 0, low < SB_SKIP)

    for s in range(group):
        def rest_body(carry, s=s):
            j = carry[0]
            return j - 1, jnp.min(visit(s, j, row_bias(j, True), lo=SB_TOP))

        def top_body(carry, s=s):
            j = carry[0]
            return j - 1, jnp.min(visit(s, j, row_bias(j, True), hi=SB_TOP))

        lax.while_loop(cond, rest_body, (iq0 + s - 2, rests[s]))
        lax.while_loop(cond, top_body, (iq0 + s - 3, tops[s]))

    for s in range(group):
        for p in pairs:
            obf_ref[s * tq:(s + 1) * tq, cols[p]] = o_ref[s * SB_PAIRS + p].astype(BF16)
    out_ref[...] = h_ref[...] + _dot(obf_ref[...], wo_ref[...])


def _sb_layer(h, q, k, v, wo, *, batch, lp):
    m = q.shape[0]
    nq = lp // SB_BLOCK
    tq = SB_BLOCK
    group = max(g for g in (5, 4, 3, 2, 1) if nq % g == 0)
    steps = nq // group
    row = lambda b, i: (b * steps + i, 0)
    whole = lambda b, i: (b, 0)
    return pl.pallas_call(
        functools.partial(_sb_kernel, group=group),
        grid=(batch, steps),
        in_specs=[pl.BlockSpec((group * tq, SB_WIDTH), row),
                  pl.BlockSpec((lp, SB_WIDTH), whole, pipeline_mode=pl.Buffered(1)),
                  pl.BlockSpec((lp, SB_WIDTH), whole, pipeline_mode=pl.Buffered(1)),
                  pl.BlockSpec((group * tq, D_MODEL), row),
                  pl.BlockSpec((SB_WIDTH, D_MODEL), lambda b, i: (0, 0), pipeline_mode=pl.Buffered(1))],
        out_specs=pl.BlockSpec((group * tq, D_MODEL), row, pipeline_mode=pl.Buffered(1)),
        out_shape=jax.ShapeDtypeStruct((m, D_MODEL), F32),
        scratch_shapes=[pltpu.VMEM((group * SB_PAIRS, 2 * tq, tq), F32),
                        pltpu.VMEM((group * SB_PAIRS, tq, LANES), F32),
                        pltpu.VMEM((group * tq, SB_WIDTH), BF16)],
        compiler_params=_params(("parallel", "arbitrary")),
        name="sb_attn",
    )(q, k, v, h, wo)


def _row_tile(lp):
    for tm in (640, 512, 384, 256, 128):
        if lp % tm == 0:
            return tm
    raise ValueError(f"padded length {lp} is not a multiple of {GCHUNK}")


def kernel(x, meta_tokens, gdn_norm_g, gdn_w_in, gdn_conv_w, gdn_a_log, gdn_dt_bias, gdn_onorm_g, gdn_w_out,
           kv_norm_g, w_kv, sb_norm_g, sb_w_q, sb_w_o, ffn_norm_g, ffn_w_gate_up, ffn_w_down, final_norm_g):
    batch, seq, _ = x.shape
    assert seq % GCHUNK == 0 and gdn_w_in.shape[0] == 1 and sb_w_q.shape[0] == 1
    lp = GCHUNK + seq
    tm = _row_tile(lp)

    assert tm > GCHUNK and seq >= tm
    head = jnp.concatenate([jnp.zeros((FRONT, D_MODEL), x.dtype), meta_tokens.astype(x.dtype)], axis=0)

    w_in = gdn_w_in[0]
    w_main = w_in[:, :4 * GDN_WIDTH].astype(BF16)
    w_ab = jnp.pad(w_in[:, 4 * GDN_WIDTH:], ((0, 0), (0, LANES - 2 * GDN_HEADS))).astype(BF16)
    ab_par = jnp.zeros((SUBLANES, LANES), F32)
    ab_par = ab_par.at[0, :GDN_HEADS].set(gdn_a_log[0]).at[1, :GDN_HEADS].set(gdn_dt_bias[0])

    q, k, v, gate, gb = _gdn_in(x, head, gdn_norm_g[0][None], w_main, w_ab, gdn_conv_w[0], ab_par, tm=tm, lp=lp)
    h = _gdn_core(q, k, v, gb, gate, x, head, gdn_onorm_g[0][None], gdn_w_out[0].astype(BF16))
    h = _ffn(h, ffn_norm_g[0][None], ffn_w_gate_up[0].astype(BF16), ffn_w_down[0].astype(BF16), tm=tm)

    sq, sk, sv = _qkv(h, kv_norm_g[None], sb_norm_g[0][None], w_kv.astype(BF16), sb_w_q[0].astype(BF16), tm=tm)
    h = _sb_layer(h, sq, sk, sv, sb_w_o[0].astype(BF16), batch=batch, lp=lp)
    return _ffn_final(h, ffn_norm_g[1][None], ffn_w_gate_up[1].astype(BF16), ffn_w_down[1].astype(BF16),
                      final_norm_g[None], batch=batch, lp=lp, seq=seq)
```

```python
import functools

import jax
import jax.numpy as jnp
from jax import lax
from jax.experimental import pallas as pl
from jax.experimental.pallas import tpu as pltpu

F32 = jnp.float32
BF16 = jnp.bfloat16

D_MODEL = 1024
N_META = 16
GDN_HEADS = 8
GDN_HEAD_DIM = 128
GDN_WIDTH = GDN_HEADS * GDN_HEAD_DIM
CONV_WIDTH = 4
SB_HEADS = 16
SB_HEAD_DIM = 64
SB_WIDTH = SB_HEADS * SB_HEAD_DIM
D_FF = 2816
EPS = 1e-6

LANES = 128
SUBLANES = 8
MXU_DIM = 256
NEG_LOG2E = -1.4426950408889634
GCHUNK = 128
FRONT = GCHUNK - N_META
SB_BLOCK = 128
SB_PAIRS = SB_WIDTH // LANES
SB_SKIP = 88.0
SB_TOP = 32
SB_MASKED = -1e30
VMEM_LIMIT = 60 * 1024 * 1024


def _dot(a, b):
    return lax.dot_general(a, b, (((1,), (0,)), ((), ())), preferred_element_type=F32)


def _dot_nt(a, b):
    return lax.dot_general(a, b, (((1,), (1,)), ((), ())), preferred_element_type=F32)


def _split(a):
    hi = a.astype(BF16)
    lo = (a - hi.astype(F32)).astype(BF16)
    return hi, lo


def _dot_exact_lhs(m_bf16, b):
    n = b.shape[1]
    prod = _dot(m_bf16, jnp.concatenate(_split(b), axis=1))
    return prod[:, :n] + prod[:, n:]


def _rms(x, g):
    return x * lax.rsqrt(jnp.mean(x * x, axis=-1, keepdims=True) + EPS) * g


def _sigmoid(x):
    return 1.0 / (1.0 + jnp.exp2(x * NEG_LOG2E))


def _softplus(x):
    return jnp.maximum(x, 0.0) + jnp.log(1.0 + jnp.exp2(jnp.abs(x) * NEG_LOG2E))


def _params(sem):
    return pltpu.CompilerParams(dimension_semantics=sem, vmem_limit_bytes=VMEM_LIMIT)


def _gdn_in_kernel(x_ref, head_ref, ng_ref, w_ref, wab_ref, cw_ref, ab_ref,
                   q_ref, k_ref, v_ref, gate_ref, gb_ref,
                   carry_ref, *, tm):
    tb = pl.program_id(1)

    @pl.when(tb == 0)
    def _():
        carry_ref[...] = jnp.zeros_like(carry_ref)

    xt = x_ref[0]
    h = jnp.where(tb == 0, jnp.concatenate([head_ref[...], xt[:tm - GCHUNK]], axis=0), xt)
    xn = _rms(h, ng_ref[...]).astype(BF16)
    outs = (q_ref, k_ref, v_ref)

    def project(part):
        return _dot(xn, w_ref[:, part * GDN_WIDTH:(part + 1) * GDN_WIDTH])

    pending = project(0)
    for part in range(3):
        cols = slice(part * GDN_WIDTH, (part + 1) * GDN_WIDTH)
        p = pending
        pending = project(part + 1)
        ext = jnp.concatenate([carry_ref[part], p], axis=0)
        carry_ref[part] = p[tm - SUBLANES:, :]
        w0, w1, w2, w3 = (cw_ref[i:i + 1, cols] for i in range(CONV_WIDTH))
        prev = pltpu.roll(ext, 1, axis=0)
        u = ext * w1 + prev * w0
        y = p * w3 + prev[SUBLANES:] * w2 + pltpu.roll(u, 2, axis=0)[SUBLANES:]
        y = y * _sigmoid(y)
        if part < 2:
            for hd in range(GDN_HEADS):
                sl = slice(hd * GDN_HEAD_DIM, (hd + 1) * GDN_HEAD_DIM)
                seg = y[:, sl]
                outs[part][:, sl] = seg * lax.rsqrt(jnp.sum(seg * seg, axis=-1, keepdims=True) + EPS)
        else:
            v_ref[...] = y
    gate_ref[...] = pending

    ab = _dot(xn, wab_ref[...])
    a_log = ab_ref[0:1, :]
    dt_bias = ab_ref[1:2, :]
    g = -jnp.exp(a_log) * _softplus(ab + dt_bias)
    beta = _sigmoid(ab)
    lane = lax.broadcasted_iota(jnp.int32, ab.shape, 1)
    pos = tb * tm + lax.broadcasted_iota(jnp.int32, ab.shape, 0)
    gb_ref[...] = jnp.where(pos >= FRONT, jnp.where(lane < GDN_HEADS, g, beta), 0.0)


def _gdn_in(x, head, ng, w_main, w_ab, conv_w, ab_par, *, tm, lp):
    batch = x.shape[0]
    m = batch * lp
    tiles = lp // tm
    row = lambda b, t: (b * tiles + t, 0)
    const = lambda b, t: (0, 0)
    wide = pl.BlockSpec((tm, GDN_WIDTH), row)
    return pl.pallas_call(
        functools.partial(_gdn_in_kernel, tm=tm),
        grid=(batch, tiles),
        in_specs=[
            pl.BlockSpec((pl.Element(1), pl.Element(tm), pl.Element(D_MODEL)),
                         lambda b, t: (b, pl.multiple_of(jnp.maximum(t * tm - GCHUNK, 0), GCHUNK), 0)),
            pl.BlockSpec((GCHUNK, D_MODEL), const),
            pl.BlockSpec((1, D_MODEL), const),
            pl.BlockSpec((D_MODEL, 4 * GDN_WIDTH), const),
            pl.BlockSpec((D_MODEL, LANES), const),
            pl.BlockSpec((CONV_WIDTH, 3 * GDN_WIDTH), const),
            pl.BlockSpec((SUBLANES, LANES), const),
        ],
        out_specs=[wide, wide, wide, wide, pl.BlockSpec((tm, LANES), row)],
        out_shape=[jax.ShapeDtypeStruct((m, GDN_WIDTH), F32)] * 4 + [jax.ShapeDtypeStruct((m, LANES), F32)],
        scratch_shapes=[pltpu.VMEM((3, SUBLANES, GDN_WIDTH), F32)],
        compiler_params=_params(("arbitrary", "arbitrary")),
        name="gdn_in",
    )(x, head, ng, w_main, w_ab, conv_w, ab_par)


def _mm(xs, ys):
    return [_dot(x.astype(BF16), y.astype(BF16)) for x, y in zip(xs, ys)]


def _unit_lower_inverses(ns, blk_same):
    c = ns[0].shape[0]
    eye = (lax.broadcasted_iota(jnp.int32, (c, c), 0) == lax.broadcasted_iota(jnp.int32, (c, c), 1)).astype(F32)
    nd = [jnp.where(blk_same, n, 0.0) for n in ns]
    no = [n - d for n, d in zip(ns, nd)]
    nd2 = _mm(nd, nd)
    nd4 = _mm(nd2, nd2)
    nd8 = _mm(nd4, nd4)
    dinv = _mm([eye - x for x in nd], [eye + x for x in nd2])
    dinv = _mm(dinv, [eye + x for x in nd4])
    dinv = _mm(dinv, [eye + x for x in nd8])
    m1 = _mm(dinv, no)
    m2 = _mm(m1, m1)
    m4 = _mm(m2, m2)
    outer = _mm([eye - x for x in m1], [eye + x for x in m2])
    outer = _mm(outer, [eye + x for x in m4])
    t0 = _mm(outer, dinv)
    nt = [_dot(n.astype(BF16), jnp.concatenate(_split(t), axis=1)) for n, t in zip(ns, t0)]
    res = [(eye - t) - (p[:, :c] + p[:, c:]) for t, p in zip(t0, nt)]
    return [t + r for t, r in zip(t0, _mm(t0, res))]


def _gdn_core_kernel(q_ref, k_ref, v_ref, gb_ref, gate_ref, x_ref, head_ref, og_ref, wo_ref,
                     out_ref, s_ref, o_ref, *, batch):
    chunk = pl.program_id(0)

    @pl.when(chunk == 0)
    def _():
        s_ref[...] = jnp.zeros_like(s_ref)

    c = GCHUNK
    ri = lax.broadcasted_iota(jnp.int32, (c, c), 0)
    ci = lax.broadcasted_iota(jnp.int32, (c, c), 1)
    incl = ri >= ci
    strict = ri > ci
    blk_same = (ri // 16) == (ci // 16)
    tril = incl.astype(BF16)
    scale = GDN_HEAD_DIM ** -0.5
    heads = [(b, hd) for b in range(batch) for hd in range(GDN_HEADS)]
    sl = lambda hd: slice(hd * GDN_HEAD_DIM, (hd + 1) * GDN_HEAD_DIM)
    ks = [k_ref[b, :, sl(hd)] for b, hd in heads]
    kbs = [k.astype(BF16) for k in ks]
    bmats = [jnp.broadcast_to(gb_ref[b, :, GDN_HEADS + hd:GDN_HEADS + hd + 1], (c, c)) for b, hd in heads]
    gcols = [_dot_exact_lhs(tril, jnp.broadcast_to(gb_ref[b, :, hd:hd + 1], (c, c))) for b, hd in heads]
    qs = [q_ref[b, :, sl(hd)] * scale for b, hd in heads]
    kqks = [_dot_nt(jnp.concatenate([kb, q.astype(BF16)], axis=0), kb) for kb, q in zip(kbs, qs)]
    decays = [jnp.where(incl, jnp.exp(jnp.where(incl, g - g.T, 0.0)), 0.0) for g in gcols]
    ns = [jnp.where(strict, bm * d * kqk[:c], 0.0) for bm, d, kqk in zip(bmats, decays, kqks)]
    ts = _unit_lower_inverses(ns, blk_same)
    egs = [jnp.exp(g) for g in gcols]
    rhs = [jnp.concatenate([bm * v_ref[b, :, sl(hd)], (bm * eg) * k], axis=1).astype(BF16)
           for bm, eg, k, (b, hd) in zip(bmats, egs, ks, heads)]
    sols = [_dot(t.astype(BF16), r) for t, r in zip(ts, rhs)]
    ats = [kqk[c:] * d for kqk, d in zip(kqks, decays)]
    glasts = [g[c - 1:c, :] for g in gcols]
    wcs = [s[:, GDN_HEAD_DIM:].astype(BF16) for s in sols]
    qds = [(q * eg).astype(BF16) for q, eg in zip(qs, egs)]
    kdts = [(k * jnp.exp(gl - g)).T.astype(BF16) for k, gl, g in zip(ks, glasts, gcols)]
    atbs = [a.astype(BF16) for a in ats]

    ss = [s_ref[i] for i in range(len(heads))]
    sbs = [s.astype(BF16) for s in ss]
    ps = [_dot(jnp.concatenate([wc, qd], axis=0), sb) for wc, qd, sb in zip(wcs, qds, sbs)]
    ubs = [(sol[:, :GDN_HEAD_DIM] - p[:c]).astype(BF16) for sol, p in zip(sols, ps)]
    rs = [_dot(jnp.concatenate([at, kdt], axis=0), ub) for at, kdt, ub in zip(atbs, kdts, ubs)]
    for i in range(len(heads)):
        s_ref[i] = ss[i] * jnp.exp(glasts[i]) + rs[i][c:]
    for i, (b, hd) in enumerate(heads):
        o = ps[i][c:] + rs[i][:c]
        o = o * lax.rsqrt(jnp.mean(o * o, axis=-1, keepdims=True) + EPS) * og_ref[...]
        gate = gate_ref[b, :, sl(hd)]
        o_ref[b * c:(b + 1) * c, sl(hd)] = (o * (gate * _sigmoid(gate))).astype(BF16)
    resid = jnp.where(chunk == 0, jnp.concatenate([head_ref[...]] * batch, axis=0),
                      x_ref[...].reshape(batch * c, D_MODEL))
    out_ref[...] = (resid + _dot(o_ref[...], wo_ref[...])).reshape(batch, c, D_MODEL)


def _gdn_core(q, k, v, gb, gate, x, head, og, wo):
    m = q.shape[0]
    batch = x.shape[0]
    nch = m // GCHUNK // batch
    per_batch = lambda a: a.reshape(batch, m // batch, a.shape[-1])
    row = lambda c: (0, c, 0)
    const = lambda c: (0, 0)
    wide = pl.BlockSpec((batch, GCHUNK, GDN_WIDTH), row)
    out = pl.pallas_call(
        functools.partial(_gdn_core_kernel, batch=batch),
        grid=(nch,),
        in_specs=[wide, wide, wide,
                  pl.BlockSpec((batch, GCHUNK, LANES), row),
                  wide,
                  pl.BlockSpec((batch, GCHUNK, D_MODEL), lambda c: (0, jnp.maximum(c - 1, 0), 0)),
                  pl.BlockSpec((GCHUNK, D_MODEL), const),
                  pl.BlockSpec((1, GDN_HEAD_DIM), const),
                  pl.BlockSpec((GDN_WIDTH, D_MODEL), const)],
        out_specs=pl.BlockSpec((batch, GCHUNK, D_MODEL), row),
        out_shape=jax.ShapeDtypeStruct((batch, m // batch, D_MODEL), F32),
        scratch_shapes=[pltpu.VMEM((batch * GDN_HEADS, GDN_HEAD_DIM, GDN_HEAD_DIM), F32),
                        pltpu.VMEM((batch * GCHUNK, GDN_WIDTH), BF16)],
        compiler_params=_params(("arbitrary",)),
        name="gdn_core",
    )(per_batch(q), per_batch(k), per_batch(v), per_batch(gb), per_batch(gate), x, head, og, wo)
    return out.reshape(m, D_MODEL)


def _ffn_chunks():
    step = 3 * MXU_DIM
    return [(lo, min(lo + step, D_FF)) for lo in range(0, D_FF, step)]


def _ffn_body(x, g_ref, wgu_ref, wd_ref):
    xn = _rms(x, g_ref[...]).astype(BF16)

    def gate_up(lo, hi):
        return _dot(xn, wgu_ref[:, lo:hi]), _dot(xn, wgu_ref[:, D_FF + lo:D_FF + hi])

    chunks = _ffn_chunks()
    acc = x
    pending = gate_up(*chunks[0])
    for i, (lo, hi) in enumerate(chunks):
        gate, up = pending
        if i + 1 < len(chunks):
            pending = gate_up(*chunks[i + 1])
        act = (gate * _sigmoid(gate) * up).astype(BF16)
        acc = acc + _dot(act, wd_ref[lo:hi, :])
    return acc


def _ffn_kernel(x_ref, g_ref, wgu_ref, wd_ref, out_ref):
    out_ref[...] = _ffn_body(x_ref[...], g_ref, wgu_ref, wd_ref)


def _ffn_final_kernel(x_ref, g_ref, wgu_ref, wd_ref, gf_ref, out_ref):
    out_ref[0] = _rms(_ffn_body(x_ref[...], g_ref, wgu_ref, wd_ref), gf_ref[...])


def _ffn_weight_specs(const):
    return [pl.BlockSpec((1, D_MODEL), const),
            pl.BlockSpec((D_MODEL, 2 * D_FF), const, pipeline_mode=pl.Buffered(1)),
            pl.BlockSpec((D_FF, D_MODEL), const, pipeline_mode=pl.Buffered(1))]


def _ffn(x, g, w_gu, w_d, *, tm):
    m = x.shape[0]
    row = lambda r: (r, 0)
    return pl.pallas_call(
        _ffn_kernel,
        grid=(m // tm,),
        in_specs=[pl.BlockSpec((tm, D_MODEL), row)] + _ffn_weight_specs(lambda r: (0, 0)),
        out_specs=pl.BlockSpec((tm, D_MODEL), row),
        out_shape=jax.ShapeDtypeStruct((m, D_MODEL), F32),
        compiler_params=_params(("parallel",)),
        name="ffn",
    )(x, g, w_gu, w_d)


def _ffn_final(x, g, w_gu, w_d, gf, *, batch, lp, seq):
    rows = max(r for r in (512, 256, 128) if seq % r == 0)
    const = lambda b, r: (0, 0)
    return pl.pallas_call(
        _ffn_final_kernel,
        grid=(batch, seq // rows),
        in_specs=[pl.BlockSpec((pl.Element(rows), pl.Element(D_MODEL)),
                               lambda b, r: (pl.multiple_of(b * lp + GCHUNK + r * rows, GCHUNK), 0))]
        + _ffn_weight_specs(const) + [pl.BlockSpec((1, D_MODEL), const)],
        out_specs=pl.BlockSpec((1, rows, D_MODEL), lambda b, r: (b, r, 0)),
        out_shape=jax.ShapeDtypeStruct((batch, seq, D_MODEL), F32),
        compiler_params=_params(("parallel", "parallel")),
        name="ffn_final",
    )(x, g, w_gu, w_d, gf)


def _qkv_kernel(x_ref, gkv_ref, gq_ref, wkv_ref, wq_ref, q_ref, k_ref, v_ref):
    x = x_ref[...]
    inv = lax.rsqrt(jnp.mean(x * x, axis=-1, keepdims=True) + EPS)
    xkv = (x * inv * gkv_ref[...]).astype(BF16)
    xq = (x * inv * gq_ref[...]).astype(BF16)
    k_ref[...] = _dot(xkv, wkv_ref[:, :SB_WIDTH]).astype(BF16)
    v_ref[...] = _dot(xkv, wkv_ref[:, SB_WIDTH:]).astype(BF16)
    q_ref[...] = (_dot(xq, wq_ref[...]) * (SB_HEAD_DIM ** -0.5)).astype(BF16)


def _qkv(x, gkv, gq, wkv, wq, *, tm):
    m = x.shape[0]
    row = lambda r: (r, 0)
    const = lambda r: (0, 0)
    out = pl.BlockSpec((tm, SB_WIDTH), row)
    return pl.pallas_call(
        _qkv_kernel,
        grid=(m // tm,),
        in_specs=[pl.BlockSpec((tm, D_MODEL), row),
                  pl.BlockSpec((1, D_MODEL), const),
                  pl.BlockSpec((1, D_MODEL), const),
                  pl.BlockSpec((D_MODEL, 2 * SB_WIDTH), const),
                  pl.BlockSpec((D_MODEL, SB_WIDTH), const)],
        out_specs=[out, out, out],
        out_shape=[jax.ShapeDtypeStruct((m, SB_WIDTH), BF16)] * 3,
        compiler_params=_params(("parallel",)),
        name="sb_qkv",
    )(x, gkv, gq, wkv, wq)


def _sb_kernel(q_ref, k_ref, v_ref, h_ref, wo_ref, out_ref, c_ref, o_ref, obf_ref, *, group):
    tq = SB_BLOCK
    iq0 = pl.program_id(1) * group
    first = lax.broadcasted_iota(jnp.int32, (tq, LANES), 1) < SB_HEAD_DIM
    ri = lax.broadcasted_iota(jnp.int32, (tq, tq), 0)
    ci = lax.broadcasted_iota(jnp.int32, (tq, tq), 1)
    upper_ext = jnp.concatenate([(ri > ci).astype(BF16), jnp.ones((tq, LANES), BF16)], axis=1)
    pairs = range(SB_PAIRS)
    cols = [slice(p * LANES, (p + 1) * LANES) for p in pairs]

    c_ref[...] = jnp.zeros_like(c_ref)
    o_ref[...] = jnp.zeros_like(o_ref)

    def visit(s, j, bias, lo=0, hi=tq):
        start = pl.multiple_of(j * tq, tq)
        slot = [s * SB_PAIRS + p for p in pairs]
        rows = hi - lo
        first_q = lax.broadcasted_iota(jnp.int32, (rows, LANES), 1) < SB_HEAD_DIM

        def stacked_q(p):
            q2 = q_ref[s * tq + lo:s * tq + hi, cols[p]]
            zero = jnp.zeros_like(q2)
            return jnp.concatenate([jnp.where(first_q, q2, zero), jnp.where(first_q, zero, q2)], axis=0)

        zs = [_dot_nt(stacked_q(p), k_ref[pl.ds(start, tq), cols[p]]) + bias for p in pairs]
        sps = [_softplus(z) for z in zs]
        accs = [_dot(sp.astype(BF16), upper_ext) for sp in sps]
        ws = []
        cmin = None
        for p in pairs:
            c = jnp.concatenate([c_ref[slot[p], lo:hi, :], c_ref[slot[p], tq + lo:tq + hi, :]], axis=0)
            ws.append(jnp.exp(zs[p] - sps[p] - accs[p][:, :tq] - c).astype(BF16))
            c = c + accs[p][:, tq:]
            c_ref[slot[p], lo:hi, :] = c[:rows]
            c_ref[slot[p], tq + lo:tq + hi, :] = c[rows:]
            cmin = c if cmin is None else jnp.minimum(cmin, c)
        for p in pairs:
            vj = v_ref[pl.ds(start, tq), cols[p]]
            zero = jnp.zeros_like(vj)
            v2 = jnp.concatenate([jnp.where(first, vj, zero), jnp.where(first, zero, vj)], axis=0)
            o_ref[slot[p], lo:hi, :] += _dot(jnp.concatenate([ws[p][:rows], ws[p][rows:]], axis=1), v2)
        return cmin

    def row_bias(j, live):
        kp = j * tq + lax.broadcasted_iota(jnp.int32, (1, tq), 1)
        return jnp.where(jnp.logical_and(kp >= FRONT, live), 0.0, SB_MASKED)

    for s in range(group):
        kpos = (iq0 + s) * tq + ci
        diag = jnp.where(jnp.logical_and(ci < ri, kpos >= FRONT), 0.0, SB_MASKED)
        visit(s, iq0 + s, jnp.concatenate([diag, diag], axis=0))

    rests = []
    for s in range(group):
        prev = jnp.maximum(iq0 + s - 1, 0)
        cmin = visit(s, prev, row_bias(prev, iq0 + s >= 1))
        rests.append(jnp.min(jnp.minimum(cmin[SB_TOP:tq], cmin[tq + SB_TOP:])))

    tops = []
    for s in range(group):
        prev = jnp.maximum(iq0 + s - 2, 0)
        tops.append(jnp.min(visit(s, prev, row_bias(prev, iq0 + s >= 2), hi=SB_TOP)))

    def cond(carry):
        j, low = carry
        return jnp.logical_and(j >= 0, low < SB_SKIP)

    for s in range(group):
        def rest_body(carry, s=s):
            j = carry[0]
            return j - 1, jnp.min(visit(s, j, row_bias(j, True), lo=SB_TOP))

        def top_body(carry, s=s):
            j = carry[0]
            return j - 1, jnp.min(visit(s, j, row_bias(j, True), hi=SB_TOP))

        lax.while_loop(cond, rest_body, (iq0 + s - 2, rests[s]))
        lax.while_loop(cond, top_body, (iq0 + s - 3, tops[s]))

    for s in range(group):
        for p in pairs:
            obf_ref[s * tq:(s + 1) * tq, cols[p]] = o_ref[s * SB_PAIRS + p].astype(BF16)
    out_ref[...] = h_ref[...] + _dot(obf_ref[...], wo_ref[...])


def _sb_layer(h, q, k, v, wo, *, batch, lp):
    m = q.shape[0]
    nq = lp // SB_BLOCK
    tq = SB_BLOCK
    group = max(g for g in (5, 4, 3, 2, 1) if nq % g == 0)
    steps = nq // group
    row = lambda b, i: (b * steps + i, 0)
    whole = lambda b, i: (b, 0)
    return pl.pallas_call(
        functools.partial(_sb_kernel, group=group),
        grid=(batch, steps),
        in_specs=[pl.BlockSpec((group * tq, SB_WIDTH), row),
                  pl.BlockSpec((lp, SB_WIDTH), whole, pipeline_mode=pl.Buffered(1)),
                  pl.BlockSpec((lp, SB_WIDTH), whole, pipeline_mode=pl.Buffered(1)),
                  pl.BlockSpec((group * tq, D_MODEL), row),
                  pl.BlockSpec((SB_WIDTH, D_MODEL), lambda b, i: (0, 0), pipeline_mode=pl.Buffered(1))],
        out_specs=pl.BlockSpec((group * tq, D_MODEL), row, pipeline_mode=pl.Buffered(1)),
        out_shape=jax.ShapeDtypeStruct((m, D_MODEL), F32),
        scratch_shapes=[pltpu.VMEM((group * SB_PAIRS, 2 * tq, tq), F32),
                        pltpu.VMEM((group * SB_PAIRS, tq, LANES), F32),
                        pltpu.VMEM((group * tq, SB_WIDTH), BF16)],
        compiler_params=_params(("parallel", "arbitrary")),
        name="sb_attn",
    )(q, k, v, h, wo)


def _row_tile(lp):
    for tm in (640, 512, 384, 256, 128):
        if lp % tm == 0:
            return tm
    raise ValueError(f"padded length {lp} is not a multiple of {GCHUNK}")


def kernel(x, meta_tokens, gdn_norm_g, gdn_w_in, gdn_conv_w, gdn_a_log, gdn_dt_bias, gdn_onorm_g, gdn_w_out,
           kv_norm_g, w_kv, sb_norm_g, sb_w_q, sb_w_o, ffn_norm_g, ffn_w_gate_up, ffn_w_down, final_norm_g):
    batch, seq, _ = x.shape
    assert seq % GCHUNK == 0 and gdn_w_in.shape[0] == 1 and sb_w_q.shape[0] == 1
    lp = GCHUNK + seq
    tm = _row_tile(lp)

    assert tm > GCHUNK and seq >= tm
    head = jnp.concatenate([jnp.zeros((FRONT, D_MODEL), x.dtype), meta_tokens.astype(x.dtype)], axis=0)

    w_in = gdn_w_in[0]
    w_main = w_in[:, :4 * GDN_WIDTH].astype(BF16)
    w_ab = jnp.pad(w_in[:, 4 * GDN_WIDTH:], ((0, 0), (0, LANES - 2 * GDN_HEADS))).astype(BF16)
    ab_par = jnp.zeros((SUBLANES, LANES), F32)
    ab_par = ab_par.at[0, :GDN_HEADS].set(gdn_a_log[0]).at[1, :GDN_HEADS].set(gdn_dt_bias[0])

    q, k, v, gate, gb = _gdn_in(x, head, gdn_norm_g[0][None], w_main, w_ab, gdn_conv_w[0], ab_par, tm=tm, lp=lp)
    h = _gdn_core(q, k, v, gb, gate, x, head, gdn_onorm_g[0][None], gdn_w_out[0].astype(BF16))
    h = _ffn(h, ffn_norm_g[0][None], ffn_w_gate_up[0].astype(BF16), ffn_w_down[0].astype(BF16), tm=tm)

    sq, sk, sv = _qkv(h, kv_norm_g[None], sb_norm_g[0][None], w_kv.astype(BF16), sb_w_q[0].astype(BF16), tm=tm)
    h = _sb_layer(h, sq, sk, sv, sb_w_o[0].astype(BF16), batch=batch, lp=lp)
    return _ffn_final(h, ffn_norm_g[1][None], ffn_w_gate_up[1].astype(BF16), ffn_w_down[1].astype(BF16),
                      final_norm_g[None], batch=batch, lp=lp, seq=seq)
```

```python
import functools

import jax
import jax.numpy as jnp
from jax import lax
from jax.experimental import pallas as pl
from jax.experimental.pallas import tpu as pltpu

F32 = jnp.float32
BF16 = jnp.bfloat16

D_MODEL = 1024
N_META = 16
GDN_HEADS = 8
GDN_HEAD_DIM = 128
GDN_WIDTH = GDN_HEADS * GDN_HEAD_DIM
CONV_WIDTH = 4
SB_HEADS = 16
SB_HEAD_DIM = 64
SB_WIDTH = SB_HEADS * SB_HEAD_DIM
D_FF = 2816
EPS = 1e-6

LANES = 128
SUBLANES = 8
MXU_DIM = 256
NEG_LOG2E = -1.4426950408889634
GCHUNK = 128
FRONT = GCHUNK - N_META
INV_BLOCK = 16
CAST_BLOCK_BYTES = 6 * 1024 * 1024
SB_BLOCK = 128
SB_PAIRS = SB_WIDTH // LANES
SB_SKIP = 88.0
SB_TOP = 32
SB_MASKED = -1e30
VMEM_LIMIT = 60 * 1024 * 1024


def _dot(a, b):
    return lax.dot_general(a, b, (((1,), (0,)), ((), ())), preferred_element_type=F32)


def _dot_nt(a, b):
    return lax.dot_general(a, b, (((1,), (1,)), ((), ())), preferred_element_type=F32)


def _split(a):
    hi = a.astype(BF16)
    lo = (a - hi.astype(F32)).astype(BF16)
    return hi, lo


def _dot_exact_lhs(m_bf16, b):
    n = b.shape[1]
    prod = _dot(m_bf16, jnp.concatenate(_split(b), axis=1))
    return prod[:, :n] + prod[:, n:]


def _rms(x, g):
    return x * lax.rsqrt(jnp.mean(x * x, axis=-1, keepdims=True) + EPS) * g


def _sigmoid(x):
    return 1.0 / (1.0 + jnp.exp2(x * NEG_LOG2E))


def _softplus(x):
    return jnp.maximum(x, 0.0) + jnp.log(1.0 + jnp.exp2(jnp.abs(x) * NEG_LOG2E))


def _params(sem):
    return pltpu.CompilerParams(dimension_semantics=sem, vmem_limit_bytes=VMEM_LIMIT)


def _cast_kernel(x_ref, o_ref):
    o_ref[...] = x_ref[...].astype(BF16)


def _to_bf16(w):
    cols = w.shape[-1]
    flat = w.reshape(-1, cols)
    rows = flat.shape[0]
    tr = max(r for r in (2048, 1024, 512, 256, 128, 64, 32, 16)
             if rows % r == 0 and r * cols * 4 <= CAST_BLOCK_BYTES)
    spec = pl.BlockSpec((tr, cols), lambda r: (r, 0))
    out = pl.pallas_call(
        _cast_kernel,
        grid=(rows // tr,),
        in_specs=[spec],
        out_specs=spec,
        out_shape=jax.ShapeDtypeStruct(flat.shape, BF16),
        compiler_params=_params(("parallel",)),
        name="to_bf16",
    )(flat)
    return out.reshape(w.shape)


def _gdn_in_kernel(x_ref, head_ref, ng_ref, w_ref, wab_ref, cw_ref, ab_ref,
                   q_ref, k_ref, v_ref, gate_ref, gb_ref,
                   carry_ref, *, tm):
    tb = pl.program_id(1)

    @pl.when(tb == 0)
    def _():
        carry_ref[...] = jnp.zeros_like(carry_ref)

    xt = x_ref[0]
    h = jnp.where(tb == 0, jnp.concatenate([head_ref[...], xt[:tm - GCHUNK]], axis=0), xt)
    xn = _rms(h, ng_ref[...]).astype(BF16)
    outs = (q_ref, k_ref, v_ref)

    def project(part):
        return _dot(xn, w_ref[:, part * GDN_WIDTH:(part + 1) * GDN_WIDTH])

    pending = project(0)
    for part in range(3):
        cols = slice(part * GDN_WIDTH, (part + 1) * GDN_WIDTH)
        p = pending
        pending = project(part + 1)
        ext = jnp.concatenate([carry_ref[part], p], axis=0)
        carry_ref[part] = p[tm - SUBLANES:, :]
        w0, w1, w2, w3 = (cw_ref[i:i + 1, cols] for i in range(CONV_WIDTH))
        prev = pltpu.roll(ext, 1, axis=0)
        u = ext * w1 + prev * w0
        y = p * w3 + prev[SUBLANES:] * w2 + pltpu.roll(u, 2, axis=0)[SUBLANES:]
        y = y * _sigmoid(y)
        if part < 2:
            for hd in range(GDN_HEADS):
                sl = slice(hd * GDN_HEAD_DIM, (hd + 1) * GDN_HEAD_DIM)
                seg = y[:, sl]
                outs[part][:, sl] = seg * lax.rsqrt(jnp.sum(seg * seg, axis=-1, keepdims=True) + EPS)
        else:
            v_ref[...] = y
    gate_ref[...] = pending

    ab = _dot(xn, wab_ref[...])
    a_log = ab_ref[0:1, :]
    dt_bias = ab_ref[1:2, :]
    g = -jnp.exp(a_log) * _softplus(ab + dt_bias)
    beta = _sigmoid(ab)
    lane = lax.broadcasted_iota(jnp.int32, ab.shape, 1)
    pos = tb * tm + lax.broadcasted_iota(jnp.int32, ab.shape, 0)
    gb_ref[...] = jnp.where(pos >= FRONT, jnp.where(lane < GDN_HEADS, g, beta), 0.0)


def _gdn_in(x, head, ng, w_main, w_ab, conv_w, ab_par, *, tm, lp):
    batch = x.shape[0]
    m = batch * lp
    tiles = lp // tm
    row = lambda b, t: (b * tiles + t, 0)
    const = lambda b, t: (0, 0)
    wide = pl.BlockSpec((tm, GDN_WIDTH), row)
    return pl.pallas_call(
        functools.partial(_gdn_in_kernel, tm=tm),
        grid=(batch, tiles),
        in_specs=[
            pl.BlockSpec((pl.Element(1), pl.Element(tm), pl.Element(D_MODEL)),
                         lambda b, t: (b, pl.multiple_of(jnp.maximum(t * tm - GCHUNK, 0), GCHUNK), 0)),
            pl.BlockSpec((GCHUNK, D_MODEL), const),
            pl.BlockSpec((1, D_MODEL), const),
            pl.BlockSpec((D_MODEL, 4 * GDN_WIDTH), const),
            pl.BlockSpec((D_MODEL, LANES), const),
            pl.BlockSpec((CONV_WIDTH, 3 * GDN_WIDTH), const),
            pl.BlockSpec((SUBLANES, LANES), const),
        ],
        out_specs=[wide, wide, wide, wide, pl.BlockSpec((tm, LANES), row)],
        out_shape=[jax.ShapeDtypeStruct((m, GDN_WIDTH), F32)] * 4 + [jax.ShapeDtypeStruct((m, LANES), F32)],
        scratch_shapes=[pltpu.VMEM((3, SUBLANES, GDN_WIDTH), F32)],
        compiler_params=_params(("arbitrary", "arbitrary")),
        name="gdn_in",
    )(x, head, ng, w_main, w_ab, conv_w, ab_par)


def _mm(xs, ys):
    return [_dot(x.astype(BF16), y.astype(BF16)) for x, y in zip(xs, ys)]


def _unit_lower_inverses(ns, blk_same):
    c = ns[0].shape[0]
    eye = (lax.broadcasted_iota(jnp.int32, (c, c), 0) == lax.broadcasted_iota(jnp.int32, (c, c), 1)).astype(F32)
    nd = [jnp.where(blk_same, n, 0.0) for n in ns]
    no = [n - d for n, d in zip(ns, nd)]
    nd2 = _mm(nd, nd)
    nd4 = _mm(nd2, nd2)
    nd8 = _mm(nd4, nd4)
    dinv = _mm([eye - x for x in nd], [eye + x for x in nd2])
    dinv = _mm(dinv, [eye + x for x in nd4])
    dinv = _mm(dinv, [eye + x for x in nd8])
    m1 = _mm(dinv, no)
    m2 = _mm(m1, m1)
    m4 = _mm(m2, m2)
    outer = _mm([eye - x for x in m1], [eye + x for x in m2])
    outer = _mm(outer, [eye + x for x in m4])
    t0 = _mm(outer, dinv)
    nt = [_dot(n.astype(BF16), jnp.concatenate(_split(t), axis=1)) for n, t in zip(ns, t0)]
    res = [(eye - t) - (p[:, :c] + p[:, c:]) for t, p in zip(t0, nt)]
    return [t + r for t, r in zip(t0, _mm(t0, res))]


def _gdn_core_kernel(q_ref, k_ref, v_ref, gb_ref, gate_ref, x_ref, head_ref, og_ref, wo_ref,
                     out_ref, s_ref, o_ref, *, batch):
    chunk = pl.program_id(0)

    @pl.when(chunk == 0)
    def _():
        s_ref[...] = jnp.zeros_like(s_ref)

    c = GCHUNK
    ri = lax.broadcasted_iota(jnp.int32, (c, c), 0)
    ci = lax.broadcasted_iota(jnp.int32, (c, c), 1)
    incl = ri >= ci
    strict = ri > ci
    blk_same = (ri // INV_BLOCK) == (ci // INV_BLOCK)
    tril = incl.astype(BF16)
    scale = GDN_HEAD_DIM ** -0.5
    heads = [(b, hd) for b in range(batch) for hd in range(GDN_HEADS)]
    sl = lambda hd: slice(hd * GDN_HEAD_DIM, (hd + 1) * GDN_HEAD_DIM)
    ks = [k_ref[b, :, sl(hd)] for b, hd in heads]
    kbs = [k.astype(BF16) for k in ks]
    bmats = [jnp.broadcast_to(gb_ref[b, :, GDN_HEADS + hd:GDN_HEADS + hd + 1], (c, c)) for b, hd in heads]
    gcols = [_dot_exact_lhs(tril, jnp.broadcast_to(gb_ref[b, :, hd:hd + 1], (c, c))) for b, hd in heads]
    qs = [q_ref[b, :, sl(hd)] * scale for b, hd in heads]
    kqks = [_dot_nt(jnp.concatenate([kb, q.astype(BF16)], axis=0), kb) for kb, q in zip(kbs, qs)]
    decays = [jnp.where(incl, jnp.exp(jnp.where(incl, g - g.T, 0.0)), 0.0) for g in gcols]
    ns = [jnp.where(strict, bm * d * kqk[:c], 0.0) for bm, d, kqk in zip(bmats, decays, kqks)]
    ts = _unit_lower_inverses(ns, blk_same)
    egs = [jnp.exp(g) for g in gcols]
    rhs = [jnp.concatenate([bm * v_ref[b, :, sl(hd)], (bm * eg) * k], axis=1).astype(BF16)
           for bm, eg, k, (b, hd) in zip(bmats, egs, ks, heads)]
    sols = [_dot(t.astype(BF16), r) for t, r in zip(ts, rhs)]
    ats = [kqk[c:] * d for kqk, d in zip(kqks, decays)]
    glasts = [g[c - 1:c, :] for g in gcols]
    wcs = [s[:, GDN_HEAD_DIM:].astype(BF16) for s in sols]
    qds = [(q * eg).astype(BF16) for q, eg in zip(qs, egs)]
    kdts = [(k * jnp.exp(gl - g)).T.astype(BF16) for k, gl, g in zip(ks, glasts, gcols)]
    atbs = [a.astype(BF16) for a in ats]

    ss = [s_ref[i] for i in range(len(heads))]
    sbs = [s.astype(BF16) for s in ss]
    ps = [_dot(jnp.concatenate([wc, qd], axis=0), sb) for wc, qd, sb in zip(wcs, qds, sbs)]
    ubs = [(sol[:, :GDN_HEAD_DIM] - p[:c]).astype(BF16) for sol, p in zip(sols, ps)]
    rs = [_dot(jnp.concatenate([at, kdt], axis=0), ub) for at, kdt, ub in zip(atbs, kdts, ubs)]
    for i in range(len(heads)):
        s_ref[i] = ss[i] * jnp.exp(glasts[i]) + rs[i][c:]
    for i, (b, hd) in enumerate(heads):
        o = ps[i][c:] + rs[i][:c]
        o = o * lax.rsqrt(jnp.mean(o * o, axis=-1, keepdims=True) + EPS) * og_ref[...]
        gate = gate_ref[b, :, sl(hd)]
        o_ref[b * c:(b + 1) * c, sl(hd)] = (o * (gate * _sigmoid(gate))).astype(BF16)
    resid = jnp.where(chunk == 0, jnp.concatenate([head_ref[...]] * batch, axis=0),
                      x_ref[...].reshape(batch * c, D_MODEL))
    out_ref[...] = (resid + _dot(o_ref[...], wo_ref[...])).reshape(batch, c, D_MODEL)


def _gdn_core(q, k, v, gb, gate, x, head, og, wo):
    m = q.shape[0]
    batch = x.shape[0]
    nch = m // GCHUNK // batch
    per_batch = lambda a: a.reshape(batch, m // batch, a.shape[-1])
    row = lambda c: (0, c, 0)
    const = lambda c: (0, 0)
    wide = pl.BlockSpec((batch, GCHUNK, GDN_WIDTH), row)
    out = pl.pallas_call(
        functools.partial(_gdn_core_kernel, batch=batch),
        grid=(nch,),
        in_specs=[wide, wide, wide,
                  pl.BlockSpec((batch, GCHUNK, LANES), row),
                  wide,
                  pl.BlockSpec((batch, GCHUNK, D_MODEL), lambda c: (0, jnp.maximum(c - 1, 0), 0)),
                  pl.BlockSpec((GCHUNK, D_MODEL), const),
                  pl.BlockSpec((1, GDN_HEAD_DIM), const),
                  pl.BlockSpec((GDN_WIDTH, D_MODEL), const)],
        out_specs=pl.BlockSpec((batch, GCHUNK, D_MODEL), row),
        out_shape=jax.ShapeDtypeStruct((batch, m // batch, D_MODEL), F32),
        scratch_shapes=[pltpu.VMEM((batch * GDN_HEADS, GDN_HEAD_DIM, GDN_HEAD_DIM), F32),
                        pltpu.VMEM((batch * GCHUNK, GDN_WIDTH), BF16)],
        compiler_params=_params(("arbitrary",)),
        name="gdn_core",
    )(per_batch(q), per_batch(k), per_batch(v), per_batch(gb), per_batch(gate), x, head, og, wo)
    return out.reshape(m, D_MODEL)


def _ffn_chunks():
    step = 3 * MXU_DIM
    return [(lo, min(lo + step, D_FF)) for lo in range(0, D_FF, step)]


def _ffn_body(x, g_ref, wgu_ref, wd_ref):
    xn = _rms(x, g_ref[...]).astype(BF16)

    def gate_up(lo, hi):
        return _dot(xn, wgu_ref[:, lo:hi]), _dot(xn, wgu_ref[:, D_FF + lo:D_FF + hi])

    chunks = _ffn_chunks()
    acc = x
    pending = gate_up(*chunks[0])
    for i, (lo, hi) in enumerate(chunks):
        gate, up = pending
        if i + 1 < len(chunks):
            pending = gate_up(*chunks[i + 1])
        act = (gate * _sigmoid(gate) * up).astype(BF16)
        acc = acc + _dot(act, wd_ref[lo:hi, :])
    return acc


def _ffn_kernel(x_ref, g_ref, wgu_ref, wd_ref, out_ref):
    out_ref[...] = _ffn_body(x_ref[...], g_ref, wgu_ref, wd_ref)


def _ffn_final_kernel(x_ref, g_ref, wgu_ref, wd_ref, gf_ref, out_ref):
    out_ref[0] = _rms(_ffn_body(x_ref[...], g_ref, wgu_ref, wd_ref), gf_ref[...])


def _ffn_weight_specs(layer):
    pick = lambda *_: (layer, 0, 0)
    return [pl.BlockSpec((1, D_MODEL), lambda *_: (0, 0)),
            pl.BlockSpec((None, D_MODEL, 2 * D_FF), pick, pipeline_mode=pl.Buffered(1)),
            pl.BlockSpec((None, D_FF, D_MODEL), pick, pipeline_mode=pl.Buffered(1))]


def _ffn(x, g, w_gu, w_d, *, layer, tm):
    m = x.shape[0]
    row = lambda r: (r, 0)
    return pl.pallas_call(
        _ffn_kernel,
        grid=(m // tm,),
        in_specs=[pl.BlockSpec((tm, D_MODEL), row)] + _ffn_weight_specs(layer),
        out_specs=pl.BlockSpec((tm, D_MODEL), row),
        out_shape=jax.ShapeDtypeStruct((m, D_MODEL), F32),
        compiler_params=_params(("parallel",)),
        name="ffn",
    )(x, g, w_gu, w_d)


def _ffn_final(x, g, w_gu, w_d, gf, *, layer, batch, lp, seq):
    rows = max(r for r in (1024, 512, 256, 128) if seq % r == 0)
    const = lambda b, r: (0, 0)
    return pl.pallas_call(
        _ffn_final_kernel,
        grid=(batch, seq // rows),
        in_specs=[pl.BlockSpec((pl.Element(rows), pl.Element(D_MODEL)),
                               lambda b, r: (pl.multiple_of(b * lp + GCHUNK + r * rows, GCHUNK), 0))]
        + _ffn_weight_specs(layer) + [pl.BlockSpec((1, D_MODEL), const)],
        out_specs=pl.BlockSpec((1, rows, D_MODEL), lambda b, r: (b, r, 0)),
        out_shape=jax.ShapeDtypeStruct((batch, seq, D_MODEL), F32),
        compiler_params=_params(("parallel", "parallel")),
        name="ffn_final",
    )(x, g, w_gu, w_d, gf)


def _qkv_kernel(x_ref, gkv_ref, gq_ref, wkv_ref, wq_ref, q_ref, k_ref, v_ref):
    x = x_ref[...]
    inv = lax.rsqrt(jnp.mean(x * x, axis=-1, keepdims=True) + EPS)
    xkv = (x * inv * gkv_ref[...]).astype(BF16)
    xq = (x * inv * gq_ref[...]).astype(BF16)
    k_ref[...] = _dot(xkv, wkv_ref[:, :SB_WIDTH]).astype(BF16)
    v_ref[...] = _dot(xkv, wkv_ref[:, SB_WIDTH:]).astype(BF16)
    q_ref[...] = (_dot(xq, wq_ref[...]) * (SB_HEAD_DIM ** -0.5)).astype(BF16)


def _qkv(x, gkv, gq, wkv, wq, *, tm):
    m = x.shape[0]
    row = lambda r: (r, 0)
    const = lambda r: (0, 0)
    out = pl.BlockSpec((tm, SB_WIDTH), row)
    return pl.pallas_call(
        _qkv_kernel,
        grid=(m // tm,),
        in_specs=[pl.BlockSpec((tm, D_MODEL), row),
                  pl.BlockSpec((1, D_MODEL), const),
                  pl.BlockSpec((1, D_MODEL), const),
                  pl.BlockSpec((D_MODEL, 2 * SB_WIDTH), const),
                  pl.BlockSpec((D_MODEL, SB_WIDTH), const)],
        out_specs=[out, out, out],
        out_shape=[jax.ShapeDtypeStruct((m, SB_WIDTH), BF16)] * 3,
        compiler_params=_params(("parallel",)),
        name="sb_qkv",
    )(x, gkv, gq, wkv, wq)


def _sb_kernel(q_ref, k_ref, v_ref, h_ref, wo_ref, out_ref, c_ref, o_ref, obf_ref, *, group):
    tq = SB_BLOCK
    iq0 = pl.program_id(1) * group
    first = lax.broadcasted_iota(jnp.int32, (tq, LANES), 1) < SB_HEAD_DIM
    ri = lax.broadcasted_iota(jnp.int32, (tq, tq), 0)
    ci = lax.broadcasted_iota(jnp.int32, (tq, tq), 1)
    upper_ext = jnp.concatenate([(ri > ci).astype(BF16), jnp.ones((tq, LANES), BF16)], axis=1)
    pairs = range(SB_PAIRS)
    cols = [slice(p * LANES, (p + 1) * LANES) for p in pairs]

    c_ref[...] = jnp.zeros_like(c_ref)
    o_ref[...] = jnp.zeros_like(o_ref)

    def visit(s, j, bias, lo=0, hi=tq):
        start = pl.multiple_of(j * tq, tq)
        slot = [s * SB_PAIRS + p for p in pairs]
        rows = hi - lo
        first_q = lax.broadcasted_iota(jnp.int32, (rows, LANES), 1) < SB_HEAD_DIM

        def stacked_q(p):
            q2 = q_ref[s * tq + lo:s * tq + hi, cols[p]]
            zero = jnp.zeros_like(q2)
            return jnp.concatenate([jnp.where(first_q, q2, zero), jnp.where(first_q, zero, q2)], axis=0)

        zs = [_dot_nt(stacked_q(p), k_ref[pl.ds(start, tq), cols[p]]) + bias for p in pairs]
        sps = [_softplus(z) for z in zs]
        accs = [_dot(sp.astype(BF16), upper_ext) for sp in sps]
        ws = []
        cmin = None
        for p in pairs:
            c = jnp.concatenate([c_ref[slot[p], lo:hi, :], c_ref[slot[p], tq + lo:tq + hi, :]], axis=0)
            ws.append(jnp.exp(zs[p] - sps[p] - accs[p][:, :tq] - c).astype(BF16))
            c = c + accs[p][:, tq:]
            c_ref[slot[p], lo:hi, :] = c[:rows]
            c_ref[slot[p], tq + lo:tq + hi, :] = c[rows:]
            cmin = c if cmin is None else jnp.minimum(cmin, c)
        for p in pairs:
            vj = v_ref[pl.ds(start, tq), cols[p]]
            zero = jnp.zeros_like(vj)
            v2 = jnp.concatenate([jnp.where(first, vj, zero), jnp.where(first, zero, vj)], axis=0)
            o_ref[slot[p], lo:hi, :] += _dot(jnp.concatenate([ws[p][:rows], ws[p][rows:]], axis=1), v2)
        return cmin

    def row_bias(j, live):
        kp = j * tq + lax.broadcasted_iota(jnp.int32, (1, tq), 1)
        return jnp.where(jnp.logical_and(kp >= FRONT, live), 0.0, SB_MASKED)

    for s in range(group):
        kpos = (iq0 + s) * tq + ci
        diag = jnp.where(jnp.logical_and(ci < ri, kpos >= FRONT), 0.0, SB_MASKED)
        visit(s, iq0 + s, jnp.concatenate([diag, diag], axis=0))

    rests = []
    for s in range(group):
        prev = jnp.maximum(iq0 + s - 1, 0)
        cmin = visit(s, prev, row_bias(prev, iq0 + s >= 1))
        rests.append(jnp.min(jnp.minimum(cmin[SB_TOP:tq], cmin[tq + SB_TOP:])))

    tops = []
    for s in range(group):
        prev = jnp.maximum(iq0 + s - 2, 0)
        tops.append(jnp.min(visit(s, prev, row_bias(prev, iq0 + s >= 2), hi=SB_TOP)))

    def cond(carry):
        j, low = carry
        return jnp.logical_and(j >= 0, low < SB_SKIP)

    for s in range(group):
        def rest_body(carry, s=s):
            j = carry[0]
            return j - 1, jnp.min(visit(s, j, row_bias(j, True), lo=SB_TOP))

        def top_body(carry, s=s):
            j = carry[0]
            return j - 1, jnp.min(visit(s, j, row_bias(j, True), hi=SB_TOP))

        lax.while_loop(cond, rest_body, (iq0 + s - 2, rests[s]))
        lax.while_loop(cond, top_body, (iq0 + s - 3, tops[s]))

    for s in range(group):
        for p in pairs:
            obf_ref[s * tq:(s + 1) * tq, cols[p]] = o_ref[s * SB_PAIRS + p].astype(BF16)
    out_ref[...] = h_ref[...] + _dot(obf_ref[...], wo_ref[...])


def _sb_layer(h, q, k, v, wo, *, batch, lp):
    m = q.shape[0]
    nq = lp // SB_BLOCK
    tq = SB_BLOCK
    group = max(g for g in (5, 4, 3, 2, 1) if nq % g == 0)
    steps = nq // group
    row = lambda b, i: (b * steps + i, 0)
    whole = lambda b, i: (b, 0)
    return pl.pallas_call(
        functools.partial(_sb_kernel, group=group),
        grid=(batch, steps),
        in_specs=[pl.BlockSpec((group * tq, SB_WIDTH), row),
                  pl.BlockSpec((lp, SB_WIDTH), whole, pipeline_mode=pl.Buffered(1)),
                  pl.BlockSpec((lp, SB_WIDTH), whole, pipeline_mode=pl.Buffered(1)),
                  pl.BlockSpec((group * tq, D_MODEL), row),
                  pl.BlockSpec((SB_WIDTH, D_MODEL), lambda b, i: (0, 0), pipeline_mode=pl.Buffered(1))],
        out_specs=pl.BlockSpec((group * tq, D_MODEL), row, pipeline_mode=pl.Buffered(1)),
        out_shape=jax.ShapeDtypeStruct((m, D_MODEL), F32),
        scratch_shapes=[pltpu.VMEM((group * SB_PAIRS, 2 * tq, tq), F32),
                        pltpu.VMEM((group * SB_PAIRS, tq, LANES), F32),
                        pltpu.VMEM((group * tq, SB_WIDTH), BF16)],
        compiler_params=_params(("parallel", "arbitrary")),
        name="sb_attn",
    )(q, k, v, h, wo)


def _row_tile(lp):
    for tm in (640, 512, 384, 256, 128):
        if lp % tm == 0:
            return tm
    raise ValueError(f"padded length {lp} is not a multiple of {GCHUNK}")


def kernel(x, meta_tokens, gdn_norm_g, gdn_w_in, gdn_conv_w, gdn_a_log, gdn_dt_bias, gdn_onorm_g, gdn_w_out,
           kv_norm_g, w_kv, sb_norm_g, sb_w_q, sb_w_o, ffn_norm_g, ffn_w_gate_up, ffn_w_down, final_norm_g):
    batch, seq, _ = x.shape
    assert seq % GCHUNK == 0 and gdn_w_in.shape[0] == 1 and sb_w_q.shape[0] == 1
    lp = GCHUNK + seq
    tm = _row_tile(lp)

    assert tm > GCHUNK and seq >= tm
    head = jnp.concatenate([jnp.zeros((FRONT, D_MODEL), x.dtype), meta_tokens.astype(x.dtype)], axis=0)

    w_main = _to_bf16(gdn_w_in)[0]
    w_ab = jnp.pad(gdn_w_in[0][:, 4 * GDN_WIDTH:], ((0, 0), (0, LANES - 2 * GDN_HEADS))).astype(BF16)
    ab_par = jnp.zeros((SUBLANES, LANES), F32)
    ab_par = ab_par.at[0, :GDN_HEADS].set(gdn_a_log[0]).at[1, :GDN_HEADS].set(gdn_dt_bias[0])
    w_gu = _to_bf16(ffn_w_gate_up)
    w_down = _to_bf16(ffn_w_down)

    q, k, v, gate, gb = _gdn_in(x, head, gdn_norm_g[0][None], w_main, w_ab, gdn_conv_w[0], ab_par, tm=tm, lp=lp)
    h = _gdn_core(q, k, v, gb, gate, x, head, gdn_onorm_g[0][None], _to_bf16(gdn_w_out)[0])
    h = _ffn(h, ffn_norm_g[0][None], w_gu, w_down, layer=0, tm=tm)

    sq, sk, sv = _qkv(h, kv_norm_g[None], sb_norm_g[0][None], _to_bf16(w_kv), _to_bf16(sb_w_q)[0], tm=tm)
    h = _sb_layer(h, sq, sk, sv, _to_bf16(sb_w_o)[0], batch=batch, lp=lp)
    return _ffn_final(h, ffn_norm_g[1][None], w_gu, w_down, final_norm_g[None], layer=1, batch=batch, lp=lp, seq=seq)
```

```python
import functools

import jax
import jax.numpy as jnp
from jax import lax
from jax.experimental import pallas as pl
from jax.experimental.pallas import tpu as pltpu

F32 = jnp.float32
BF16 = jnp.bfloat16

D_MODEL = 1024
N_META = 16
GDN_HEADS = 8
GDN_HEAD_DIM = 128
GDN_WIDTH = GDN_HEADS * GDN_HEAD_DIM
CONV_WIDTH = 4
SB_HEADS = 16
SB_HEAD_DIM = 64
SB_WIDTH = SB_HEADS * SB_HEAD_DIM
D_FF = 2816
EPS = 1e-6

LANES = 128
SUBLANES = 8
MXU_DIM = 256
NEG_LOG2E = -1.4426950408889634
GCHUNK = 128
FRONT = GCHUNK - N_META
CAST_BLOCK_BYTES = 6 * 1024 * 1024
SB_BLOCK = 128
SB_PAIRS = SB_WIDTH // LANES
SB_SKIP = 88.0
SB_TOP = 32
SB_MASKED = -1e30
VMEM_LIMIT = 60 * 1024 * 1024


def _dot(a, b):
    return lax.dot_general(a, b, (((1,), (0,)), ((), ())), preferred_element_type=F32)


def _dot_nt(a, b):
    return lax.dot_general(a, b, (((1,), (1,)), ((), ())), preferred_element_type=F32)


def _split(a):
    hi = a.astype(BF16)
    lo = (a - hi.astype(F32)).astype(BF16)
    return hi, lo


def _dot_exact_lhs(m_bf16, b):
    n = b.shape[1]
    prod = _dot(m_bf16, jnp.concatenate(_split(b), axis=1))
    return prod[:, :n] + prod[:, n:]


def _rms(x, g):
    return x * lax.rsqrt(jnp.mean(x * x, axis=-1, keepdims=True) + EPS) * g


def _sigmoid(x):
    return 1.0 / (1.0 + jnp.exp2(x * NEG_LOG2E))


def _softplus(x):
    return jnp.maximum(x, 0.0) + jnp.log(1.0 + jnp.exp2(jnp.abs(x) * NEG_LOG2E))


def _params(sem):
    return pltpu.CompilerParams(dimension_semantics=sem, vmem_limit_bytes=VMEM_LIMIT)


def _cast_kernel(x_ref, o_ref):
    o_ref[...] = x_ref[...].astype(BF16)


def _to_bf16(w):
    cols = w.shape[-1]
    flat = w.reshape(-1, cols)
    rows = flat.shape[0]
    tr = max(r for r in (2048, 1024, 512, 256, 128, 64, 32, 16)
             if rows % r == 0 and r * cols * 4 <= CAST_BLOCK_BYTES)
    spec = pl.BlockSpec((tr, cols), lambda r: (r, 0))
    out = pl.pallas_call(
        _cast_kernel,
        grid=(rows // tr,),
        in_specs=[spec],
        out_specs=spec,
        out_shape=jax.ShapeDtypeStruct(flat.shape, BF16),
        compiler_params=_params(("parallel",)),
        name="to_bf16",
    )(flat)
    return out.reshape(w.shape)


def _gdn_in_kernel(x_ref, head_ref, ng_ref, w_ref, wab_ref, cw_ref, ab_ref,
                   q_ref, k_ref, v_ref, gate_ref, gb_ref,
                   carry_ref, *, tm):
    tb = pl.program_id(1)

    @pl.when(tb == 0)
    def _():
        carry_ref[...] = jnp.zeros_like(carry_ref)

    xt = x_ref[0]
    h = jnp.where(tb == 0, jnp.concatenate([head_ref[...], xt[:tm - GCHUNK]], axis=0), xt)
    xn = _rms(h, ng_ref[...]).astype(BF16)
    outs = (q_ref, k_ref, v_ref)

    def project(part):
        return _dot(xn, w_ref[:, part * GDN_WIDTH:(part + 1) * GDN_WIDTH])

    pending = project(0)
    for part in range(3):
        cols = slice(part * GDN_WIDTH, (part + 1) * GDN_WIDTH)
        p = pending
        pending = project(part + 1)
        ext = jnp.concatenate([carry_ref[part], p], axis=0)
        carry_ref[part] = p[tm - SUBLANES:, :]
        w0, w1, w2, w3 = (cw_ref[i:i + 1, cols] for i in range(CONV_WIDTH))
        prev = pltpu.roll(ext, 1, axis=0)
        u = ext * w1 + prev * w0
        y = p * w3 + prev[SUBLANES:] * w2 + pltpu.roll(u, 2, axis=0)[SUBLANES:]
        y = y * _sigmoid(y)
        if part < 2:
            for hd in range(GDN_HEADS):
                sl = slice(hd * GDN_HEAD_DIM, (hd + 1) * GDN_HEAD_DIM)
                seg = y[:, sl]
                outs[part][:, sl] = seg * lax.rsqrt(jnp.sum(seg * seg, axis=-1, keepdims=True) + EPS)
        else:
            v_ref[...] = y
    gate_ref[...] = pending

    ab = _dot(xn, wab_ref[...])
    a_log = ab_ref[0:1, :]
    dt_bias = ab_ref[1:2, :]
    g = -jnp.exp(a_log) * _softplus(ab + dt_bias)
    beta = _sigmoid(ab)
    lane = lax.broadcasted_iota(jnp.int32, ab.shape, 1)
    pos = tb * tm + lax.broadcasted_iota(jnp.int32, ab.shape, 0)
    gb_ref[...] = jnp.where(pos >= FRONT, jnp.where(lane < GDN_HEADS, g, beta), 0.0)


def _gdn_in(x, head, ng, w_main, w_ab, conv_w, ab_par, *, tm, lp):
    batch = x.shape[0]
    m = batch * lp
    tiles = lp // tm
    row = lambda b, t: (b * tiles + t, 0)
    const = lambda b, t: (0, 0)
    wide = pl.BlockSpec((tm, GDN_WIDTH), row)
    return pl.pallas_call(
        functools.partial(_gdn_in_kernel, tm=tm),
        grid=(batch, tiles),
        in_specs=[
            pl.BlockSpec((pl.Element(1), pl.Element(tm), pl.Element(D_MODEL)),
                         lambda b, t: (b, pl.multiple_of(jnp.maximum(t * tm - GCHUNK, 0), GCHUNK), 0)),
            pl.BlockSpec((GCHUNK, D_MODEL), const),
            pl.BlockSpec((1, D_MODEL), const),
            pl.BlockSpec((D_MODEL, 4 * GDN_WIDTH), const),
            pl.BlockSpec((D_MODEL, LANES), const),
            pl.BlockSpec((CONV_WIDTH, 3 * GDN_WIDTH), const),
            pl.BlockSpec((SUBLANES, LANES), const),
        ],
        out_specs=[wide, wide, wide, wide, pl.BlockSpec((tm, LANES), row)],
        out_shape=[jax.ShapeDtypeStruct((m, GDN_WIDTH), F32)] * 4 + [jax.ShapeDtypeStruct((m, LANES), F32)],
        scratch_shapes=[pltpu.VMEM((3, SUBLANES, GDN_WIDTH), F32)],
        compiler_params=_params(("arbitrary", "arbitrary")),
        name="gdn_in",
    )(x, head, ng, w_main, w_ab, conv_w, ab_par)


def _mm(xs, ys):
    return [_dot(x.astype(BF16), y.astype(BF16)) for x, y in zip(xs, ys)]


def _unit_lower_inverses(ns):
    c = ns[0].shape[0]
    ri = lax.broadcasted_iota(jnp.int32, (c, c), 0)
    ci = lax.broadcasted_iota(jnp.int32, (c, c), 1)
    eye = (ri == ci).astype(F32)
    same = lambda s: (ri // s) == (ci // s)
    size = 2
    t0 = [eye - jnp.where(same(size), n, 0.0) for n in ns]
    while size < c:
        link = jnp.logical_and(same(2 * size), jnp.logical_not(same(size)))
        tc = _mm(t0, [jnp.where(link, n, 0.0) for n in ns])
        t0 = [t - u for t, u in zip(t0, _mm(tc, t0))]
        size *= 2
    nt = [_dot(n.astype(BF16), jnp.concatenate(_split(t), axis=1)) for n, t in zip(ns, t0)]
    res = [(eye - t) - (p[:, :c] + p[:, c:]) for t, p in zip(t0, nt)]
    return [t + r for t, r in zip(t0, _mm(t0, res))]


def _gdn_core_kernel(q_ref, k_ref, v_ref, gb_ref, gate_ref, x_ref, head_ref, og_ref, wo_ref,
                     out_ref, s_ref, o_ref, *, batch):
    chunk = pl.program_id(0)

    @pl.when(chunk == 0)
    def _():
        s_ref[...] = jnp.zeros_like(s_ref)

    c = GCHUNK
    ri = lax.broadcasted_iota(jnp.int32, (c, c), 0)
    ci = lax.broadcasted_iota(jnp.int32, (c, c), 1)
    incl = ri >= ci
    strict = ri > ci
    tril = incl.astype(BF16)
    scale = GDN_HEAD_DIM ** -0.5
    heads = [(b, hd) for b in range(batch) for hd in range(GDN_HEADS)]
    sl = lambda hd: slice(hd * GDN_HEAD_DIM, (hd + 1) * GDN_HEAD_DIM)
    ks = [k_ref[b, :, sl(hd)] for b, hd in heads]
    kbs = [k.astype(BF16) for k in ks]
    bmats = [jnp.broadcast_to(gb_ref[b, :, GDN_HEADS + hd:GDN_HEADS + hd + 1], (c, c)) for b, hd in heads]
    gcols = [_dot_exact_lhs(tril, jnp.broadcast_to(gb_ref[b, :, hd:hd + 1], (c, c))) for b, hd in heads]
    qs = [q_ref[b, :, sl(hd)] * scale for b, hd in heads]
    kqks = [_dot_nt(jnp.concatenate([kb, q.astype(BF16)], axis=0), kb) for kb, q in zip(kbs, qs)]
    decays = [jnp.where(incl, jnp.exp(jnp.where(incl, g - g.T, 0.0)), 0.0) for g in gcols]
    ns = [jnp.where(strict, bm * d * kqk[:c], 0.0) for bm, d, kqk in zip(bmats, decays, kqks)]
    ts = _unit_lower_inverses(ns)
    egs = [jnp.exp(g) for g in gcols]
    rhs = [jnp.concatenate([bm * v_ref[b, :, sl(hd)], (bm * eg) * k], axis=1).astype(BF16)
           for bm, eg, k, (b, hd) in zip(bmats, egs, ks, heads)]
    sols = [_dot(t.astype(BF16), r) for t, r in zip(ts, rhs)]
    ats = [kqk[c:] * d for kqk, d in zip(kqks, decays)]
    glasts = [g[c - 1:c, :] for g in gcols]
    wcs = [s[:, GDN_HEAD_DIM:].astype(BF16) for s in sols]
    qds = [(q * eg).astype(BF16) for q, eg in zip(qs, egs)]
    kdts = [(k * jnp.exp(gl - g)).T.astype(BF16) for k, gl, g in zip(ks, glasts, gcols)]
    atbs = [a.astype(BF16) for a in ats]

    ss = [s_ref[i] for i in range(len(heads))]
    sbs = [s.astype(BF16) for s in ss]
    ps = [_dot(jnp.concatenate([wc, qd], axis=0), sb) for wc, qd, sb in zip(wcs, qds, sbs)]
    ubs = [(sol[:, :GDN_HEAD_DIM] - p[:c]).astype(BF16) for sol, p in zip(sols, ps)]
    rs = [_dot(jnp.concatenate([at, kdt], axis=0), ub) for at, kdt, ub in zip(atbs, kdts, ubs)]
    for i in range(len(heads)):
        s_ref[i] = ss[i] * jnp.exp(glasts[i]) + rs[i][c:]
    for i, (b, hd) in enumerate(heads):
        o = ps[i][c:] + rs[i][:c]
        o = o * lax.rsqrt(jnp.mean(o * o, axis=-1, keepdims=True) + EPS) * og_ref[...]
        gate = gate_ref[b, :, sl(hd)]
        o_ref[b * c:(b + 1) * c, sl(hd)] = (o * (gate * _sigmoid(gate))).astype(BF16)
    resid = jnp.where(chunk == 0, jnp.concatenate([head_ref[...]] * batch, axis=0),
                      x_ref[...].reshape(batch * c, D_MODEL))
    out_ref[...] = (resid + _dot(o_ref[...], wo_ref[...])).reshape(batch, c, D_MODEL)


def _gdn_core(q, k, v, gb, gate, x, head, og, wo):
    m = q.shape[0]
    batch = x.shape[0]
    nch = m // GCHUNK // batch
    per_batch = lambda a: a.reshape(batch, m // batch, a.shape[-1])
    row = lambda c: (0, c, 0)
    const = lambda c: (0, 0)
    wide = pl.BlockSpec((batch, GCHUNK, GDN_WIDTH), row)
    out = pl.pallas_call(
        functools.partial(_gdn_core_kernel, batch=batch),
        grid=(nch,),
        in_specs=[wide, wide, wide,
                  pl.BlockSpec((batch, GCHUNK, LANES), row),
                  wide,
                  pl.BlockSpec((batch, GCHUNK, D_MODEL), lambda c: (0, jnp.maximum(c - 1, 0), 0)),
                  pl.BlockSpec((GCHUNK, D_MODEL), const),
                  pl.BlockSpec((1, GDN_HEAD_DIM), const),
                  pl.BlockSpec((GDN_WIDTH, D_MODEL), const)],
        out_specs=pl.BlockSpec((batch, GCHUNK, D_MODEL), row),
        out_shape=jax.ShapeDtypeStruct((batch, m // batch, D_MODEL), F32),
        scratch_shapes=[pltpu.VMEM((batch * GDN_HEADS, GDN_HEAD_DIM, GDN_HEAD_DIM), F32),
                        pltpu.VMEM((batch * GCHUNK, GDN_WIDTH), BF16)],
        compiler_params=_params(("arbitrary",)),
        name="gdn_core",
    )(per_batch(q), per_batch(k), per_batch(v), per_batch(gb), per_batch(gate), x, head, og, wo)
    return out.reshape(m, D_MODEL)


def _ffn_chunks():
    step = 3 * MXU_DIM
    return [(lo, min(lo + step, D_FF)) for lo in range(0, D_FF, step)]


def _ffn_body(x, g_ref, wgu_ref, wd_ref):
    xn = _rms(x, g_ref[...]).astype(BF16)

    def gate_up(lo, hi):
        return _dot(xn, wgu_ref[:, lo:hi]), _dot(xn, wgu_ref[:, D_FF + lo:D_FF + hi])

    chunks = _ffn_chunks()
    acc = x
    pending = gate_up(*chunks[0])
    for i, (lo, hi) in enumerate(chunks):
        gate, up = pending
        if i + 1 < len(chunks):
            pending = gate_up(*chunks[i + 1])
        act = (gate * _sigmoid(gate) * up).astype(BF16)
        acc = acc + _dot(act, wd_ref[lo:hi, :])
    return acc


def _ffn_kernel(x_ref, g_ref, wgu_ref, wd_ref, out_ref):
    out_ref[...] = _ffn_body(x_ref[...], g_ref, wgu_ref, wd_ref)


def _ffn_final_kernel(x_ref, g_ref, wgu_ref, wd_ref, gf_ref, out_ref):
    out_ref[0] = _rms(_ffn_body(x_ref[...], g_ref, wgu_ref, wd_ref), gf_ref[...])


def _ffn_weight_specs(layer):
    pick = lambda *_: (layer, 0, 0)
    return [pl.BlockSpec((1, D_MODEL), lambda *_: (0, 0)),
            pl.BlockSpec((None, D_MODEL, 2 * D_FF), pick, pipeline_mode=pl.Buffered(1)),
            pl.BlockSpec((None, D_FF, D_MODEL), pick, pipeline_mode=pl.Buffered(1))]


def _ffn(x, g, w_gu, w_d, *, layer, tm):
    m = x.shape[0]
    row = lambda r: (r, 0)
    return pl.pallas_call(
        _ffn_kernel,
        grid=(m // tm,),
        in_specs=[pl.BlockSpec((tm, D_MODEL), row)] + _ffn_weight_specs(layer),
        out_specs=pl.BlockSpec((tm, D_MODEL), row),
        out_shape=jax.ShapeDtypeStruct((m, D_MODEL), F32),
        compiler_params=_params(("parallel",)),
        name="ffn",
    )(x, g, w_gu, w_d)


def _ffn_final(x, g, w_gu, w_d, gf, *, layer, batch, lp, seq):
    rows = max(r for r in (1024, 512, 256, 128) if seq % r == 0)
    const = lambda b, r: (0, 0)
    return pl.pallas_call(
        _ffn_final_kernel,
        grid=(batch, seq // rows),
        in_specs=[pl.BlockSpec((pl.Element(rows), pl.Element(D_MODEL)),
                               lambda b, r: (pl.multiple_of(b * lp + GCHUNK + r * rows, GCHUNK), 0))]
        + _ffn_weight_specs(layer) + [pl.BlockSpec((1, D_MODEL), const)],
        out_specs=pl.BlockSpec((1, rows, D_MODEL), lambda b, r: (b, r, 0)),
        out_shape=jax.ShapeDtypeStruct((batch, seq, D_MODEL), F32),
        compiler_params=_params(("parallel", "parallel")),
        name="ffn_final",
    )(x, g, w_gu, w_d, gf)


def _qkv_kernel(x_ref, gkv_ref, gq_ref, wkv_ref, wq_ref, q_ref, k_ref, v_ref):
    x = x_ref[...]
    inv = lax.rsqrt(jnp.mean(x * x, axis=-1, keepdims=True) + EPS)
    xkv = (x * inv * gkv_ref[...]).astype(BF16)
    xq = (x * inv * gq_ref[...]).astype(BF16)
    k_ref[...] = _dot(xkv, wkv_ref[:, :SB_WIDTH]).astype(BF16)
    v_ref[...] = _dot(xkv, wkv_ref[:, SB_WIDTH:]).astype(BF16)
    q_ref[...] = (_dot(xq, wq_ref[...]) * (SB_HEAD_DIM ** -0.5)).astype(BF16)


def _qkv(x, gkv, gq, wkv, wq, *, tm):
    m = x.shape[0]
    row = lambda r: (r, 0)
    const = lambda r: (0, 0)
    out = pl.BlockSpec((tm, SB_WIDTH), row)
    return pl.pallas_call(
        _qkv_kernel,
        grid=(m // tm,),
        in_specs=[pl.BlockSpec((tm, D_MODEL), row),
                  pl.BlockSpec((1, D_MODEL), const),
                  pl.BlockSpec((1, D_MODEL), const),
                  pl.BlockSpec((D_MODEL, 2 * SB_WIDTH), const),
                  pl.BlockSpec((D_MODEL, SB_WIDTH), const)],
        out_specs=[out, out, out],
        out_shape=[jax.ShapeDtypeStruct((m, SB_WIDTH), BF16)] * 3,
        compiler_params=_params(("parallel",)),
        name="sb_qkv",
    )(x, gkv, gq, wkv, wq)


def _sb_kernel(q_ref, k_ref, v_ref, h_ref, wo_ref, out_ref, c_ref, o_ref, obf_ref, *, group):
    tq = SB_BLOCK
    iq0 = pl.program_id(1) * group
    first = lax.broadcasted_iota(jnp.int32, (tq, LANES), 1) < SB_HEAD_DIM
    ri = lax.broadcasted_iota(jnp.int32, (tq, tq), 0)
    ci = lax.broadcasted_iota(jnp.int32, (tq, tq), 1)
    upper_ext = jnp.concatenate([(ri > ci).astype(BF16), jnp.ones((tq, LANES), BF16)], axis=1)
    pairs = range(SB_PAIRS)
    cols = [slice(p * LANES, (p + 1) * LANES) for p in pairs]

    c_ref[...] = jnp.zeros_like(c_ref)
    o_ref[...] = jnp.zeros_like(o_ref)

    def visit(s, j, bias, lo=0, hi=tq):
        start = pl.multiple_of(j * tq, tq)
        slot = [s * SB_PAIRS + p for p in pairs]
        rows = hi - lo
        first_q = lax.broadcasted_iota(jnp.int32, (rows, LANES), 1) < SB_HEAD_DIM

        def stacked_q(p):
            q2 = q_ref[s * tq + lo:s * tq + hi, cols[p]]
            zero = jnp.zeros_like(q2)
            return jnp.concatenate([jnp.where(first_q, q2, zero), jnp.where(first_q, zero, q2)], axis=0)

        zs = [_dot_nt(stacked_q(p), k_ref[pl.ds(start, tq), cols[p]]) + bias for p in pairs]
        sps = [_softplus(z) for z in zs]
        accs = [_dot(sp.astype(BF16), upper_ext) for sp in sps]
        ws = []
        cmin = None
        for p in pairs:
            c = jnp.concatenate([c_ref[slot[p], lo:hi, :], c_ref[slot[p], tq + lo:tq + hi, :]], axis=0)
            ws.append(jnp.exp(zs[p] - sps[p] - accs[p][:, :tq] - c).astype(BF16))
            c = c + accs[p][:, tq:]
            c_ref[slot[p], lo:hi, :] = c[:rows]
            c_ref[slot[p], tq + lo:tq + hi, :] = c[rows:]
            cmin = c if cmin is None else jnp.minimum(cmin, c)
        for p in pairs:
            vj = v_ref[pl.ds(start, tq), cols[p]]
            zero = jnp.zeros_like(vj)
            v2 = jnp.concatenate([jnp.where(first, vj, zero), jnp.where(first, zero, vj)], axis=0)
            o_ref[slot[p], lo:hi, :] += _dot(jnp.concatenate([ws[p][:rows], ws[p][rows:]], axis=1), v2)
        return cmin

    def row_bias(j, live):
        kp = j * tq + lax.broadcasted_iota(jnp.int32, (1, tq), 1)
        return jnp.where(jnp.logical_and(kp >= FRONT, live), 0.0, SB_MASKED)

    for s in range(group):
        kpos = (iq0 + s) * tq + ci
        diag = jnp.where(jnp.logical_and(ci < ri, kpos >= FRONT), 0.0, SB_MASKED)
        visit(s, iq0 + s, jnp.concatenate([diag, diag], axis=0))

    rests = []
    for s in range(group):
        prev = jnp.maximum(iq0 + s - 1, 0)
        cmin = visit(s, prev, row_bias(prev, iq0 + s >= 1))
        rests.append(jnp.min(jnp.minimum(cmin[SB_TOP:tq], cmin[tq + SB_TOP:])))

    tops = []
    for s in range(group):
        prev = jnp.maximum(iq0 + s - 2, 0)
        tops.append(jnp.min(visit(s, prev, row_bias(prev, iq0 + s >= 2), hi=SB_TOP)))

    def cond(carry):
        j, low = carry
        return jnp.logical_and(j >= 0, low < SB_SKIP)

    for s in range(group):
        def rest_body(carry, s=s):
            j = carry[0]
            return j - 1, jnp.min(visit(s, j, row_bias(j, True), lo=SB_TOP))

        def top_body(carry, s=s):
            j = carry[0]
            return j - 1, jnp.min(visit(s, j, row_bias(j, True), hi=SB_TOP))

        lax.while_loop(cond, rest_body, (iq0 + s - 2, rests[s]))
        lax.while_loop(cond, top_body, (iq0 + s - 3, tops[s]))

    for s in range(group):
        for p in pairs:
            obf_ref[s * tq:(s + 1) * tq, cols[p]] = o_ref[s * SB_PAIRS + p].astype(BF16)
    out_ref[...] = h_ref[...] + _dot(obf_ref[...], wo_ref[...])


def _sb_layer(h, q, k, v, wo, *, batch, lp):
    m = q.shape[0]
    nq = lp // SB_BLOCK
    tq = SB_BLOCK
    group = max(g for g in (5, 4, 3, 2, 1) if nq % g == 0)
    steps = nq // group
    row = lambda b, i: (b * steps + i, 0)
    whole = lambda b, i: (b, 0)
    return pl.pallas_call(
        functools.partial(_sb_kernel, group=group),
        grid=(batch, steps),
        in_specs=[pl.BlockSpec((group * tq, SB_WIDTH), row),
                  pl.BlockSpec((lp, SB_WIDTH), whole, pipeline_mode=pl.Buffered(1)),
                  pl.BlockSpec((lp, SB_WIDTH), whole, pipeline_mode=pl.Buffered(1)),
                  pl.BlockSpec((group * tq, D_MODEL), row),
                  pl.BlockSpec((SB_WIDTH, D_MODEL), lambda b, i: (0, 0), pipeline_mode=pl.Buffered(1))],
        out_specs=pl.BlockSpec((group * tq, D_MODEL), row, pipeline_mode=pl.Buffered(1)),
        out_shape=jax.ShapeDtypeStruct((m, D_MODEL), F32),
        scratch_shapes=[pltpu.VMEM((group * SB_PAIRS, 2 * tq, tq), F32),
                        pltpu.VMEM((group * SB_PAIRS, tq, LANES), F32),
                        pltpu.VMEM((group * tq, SB_WIDTH), BF16)],
        compiler_params=_params(("parallel", "arbitrary")),
        name="sb_attn",
    )(q, k, v, h, wo)


def _row_tile(lp):
    for tm in (640, 512, 384, 256, 128):
        if lp % tm == 0:
            return tm
    raise ValueError(f"padded length {lp} is not a multiple of {GCHUNK}")


def kernel(x, meta_tokens, gdn_norm_g, gdn_w_in, gdn_conv_w, gdn_a_log, gdn_dt_bias, gdn_onorm_g, gdn_w_out,
           kv_norm_g, w_kv, sb_norm_g, sb_w_q, sb_w_o, ffn_norm_g, ffn_w_gate_up, ffn_w_down, final_norm_g):
    batch, seq, _ = x.shape
    assert seq % GCHUNK == 0 and gdn_w_in.shape[0] == 1 and sb_w_q.shape[0] == 1
    lp = GCHUNK + seq
    tm = _row_tile(lp)

    assert tm > GCHUNK and seq >= tm
    head = jnp.concatenate([jnp.zeros((FRONT, D_MODEL), x.dtype), meta_tokens.astype(x.dtype)], axis=0)

    w_main = _to_bf16(gdn_w_in)[0]
    w_ab = jnp.pad(gdn_w_in[0][:, 4 * GDN_WIDTH:], ((0, 0), (0, LANES - 2 * GDN_HEADS))).astype(BF16)
    ab_par = jnp.zeros((SUBLANES, LANES), F32)
    ab_par = ab_par.at[0, :GDN_HEADS].set(gdn_a_log[0]).at[1, :GDN_HEADS].set(gdn_dt_bias[0])
    w_gu = _to_bf16(ffn_w_gate_up)
    w_down = _to_bf16(ffn_w_down)

    q, k, v, gate, gb = _gdn_in(x, head, gdn_norm_g[0][None], w_main, w_ab, gdn_conv_w[0], ab_par, tm=tm, lp=lp)
    h = _gdn_core(q, k, v, gb, gate, x, head, gdn_onorm_g[0][None], _to_bf16(gdn_w_out)[0])
    h = _ffn(h, ffn_norm_g[0][None], w_gu, w_down, layer=0, tm=tm)

    sq, sk, sv = _qkv(h, kv_norm_g[None], sb_norm_g[0][None], _to_bf16(w_kv), _to_bf16(sb_w_q)[0], tm=tm)
    h = _sb_layer(h, sq, sk, sv, _to_bf16(sb_w_o)[0], batch=batch, lp=lp)
    return _ffn_final(h, ffn_norm_g[1][None], w_gu, w_down, final_norm_g[None], layer=1, batch=batch, lp=lp, seq=seq)
```

```python
import functools

import jax
import jax.numpy as jnp
from jax import lax
from jax.experimental import pallas as pl
from jax.experimental.pallas import tpu as pltpu

F32 = jnp.float32
BF16 = jnp.bfloat16

D_MODEL = 1024
N_META = 16
GDN_HEADS = 8
GDN_HEAD_DIM = 128
GDN_WIDTH = GDN_HEADS * GDN_HEAD_DIM
CONV_WIDTH = 4
SB_HEADS = 16
SB_HEAD_DIM = 64
SB_WIDTH = SB_HEADS * SB_HEAD_DIM
D_FF = 2816
EPS = 1e-6

LANES = 128
SUBLANES = 8
MXU_DIM = 256
NEG_LOG2E = -1.4426950408889634
GCHUNK = 128
FRONT = GCHUNK - N_META
CAST_BLOCK_BYTES = 6 * 1024 * 1024
SB_BLOCK = 128
SB_PAIRS = SB_WIDTH // LANES
SB_SKIP = 88.0
SB_TOP = 32
SB_MASKED = -1e30
VMEM_LIMIT = 60 * 1024 * 1024


def _dot(a, b):
    return lax.dot_general(a, b, (((1,), (0,)), ((), ())), preferred_element_type=F32)


def _dot_nt(a, b):
    return lax.dot_general(a, b, (((1,), (1,)), ((), ())), preferred_element_type=F32)


def _split(a):
    hi = a.astype(BF16)
    lo = (a - hi.astype(F32)).astype(BF16)
    return hi, lo


def _dot_exact_lhs(m_bf16, b):
    n = b.shape[1]
    prod = _dot(m_bf16, jnp.concatenate(_split(b), axis=1))
    return prod[:, :n] + prod[:, n:]


def _rms(x, g):
    return x * lax.rsqrt(jnp.mean(x * x, axis=-1, keepdims=True) + EPS) * g


def _sigmoid(x):
    return 1.0 / (1.0 + jnp.exp2(x * NEG_LOG2E))


def _softplus(x):
    return jnp.maximum(x, 0.0) + jnp.log(1.0 + jnp.exp2(jnp.abs(x) * NEG_LOG2E))


def _params(sem):
    return pltpu.CompilerParams(dimension_semantics=sem, vmem_limit_bytes=VMEM_LIMIT)


def _cast_kernel(x_ref, o_ref):
    o_ref[...] = x_ref[...].astype(BF16)


def _to_bf16(w):
    cols = w.shape[-1]
    flat = w.reshape(-1, cols)
    rows = flat.shape[0]
    tr = max(r for r in (2048, 1024, 512, 256, 128, 64, 32, 16)
             if rows % r == 0 and r * cols * 4 <= CAST_BLOCK_BYTES)
    if tr >= LANES:
        spec, steps = pl.BlockSpec((tr, cols), lambda r: (r, 0)), rows // tr
    else:
        spec, steps = pl.BlockSpec((rows, MXU_DIM), lambda c: (0, c)), cols // MXU_DIM
    out = pl.pallas_call(
        _cast_kernel,
        grid=(steps,),
        in_specs=[spec],
        out_specs=spec,
        out_shape=jax.ShapeDtypeStruct(flat.shape, BF16),
        compiler_params=_params(("parallel",)),
        name="to_bf16",
    )(flat)
    return out.reshape(w.shape)


def _gdn_in_kernel(x_ref, head_ref, ng_ref, w_ref, wab_ref, cw_ref, ab_ref,
                   q_ref, k_ref, v_ref, gate_ref, gb_ref,
                   carry_ref, *, tm):
    tb = pl.program_id(1)

    @pl.when(tb == 0)
    def _():
        carry_ref[...] = jnp.zeros_like(carry_ref)

    xt = x_ref[0]
    h = jnp.where(tb == 0, jnp.concatenate([head_ref[...], xt[:tm - GCHUNK]], axis=0), xt)
    xn = _rms(h, ng_ref[...]).astype(BF16)
    outs = (q_ref, k_ref, v_ref)

    def project(part):
        return _dot_nt(xn, w_ref[part * GDN_WIDTH:(part + 1) * GDN_WIDTH, :])

    pending = project(0)
    for part in range(3):
        cols = slice(part * GDN_WIDTH, (part + 1) * GDN_WIDTH)
        p = pending
        pending = project(part + 1)
        ext = jnp.concatenate([carry_ref[part], p], axis=0)
        carry_ref[part] = p[tm - SUBLANES:, :]
        w0, w1, w2, w3 = (cw_ref[i:i + 1, cols] for i in range(CONV_WIDTH))
        prev = pltpu.roll(ext, 1, axis=0)
        u = ext * w1 + prev * w0
        y = p * w3 + prev[SUBLANES:] * w2 + pltpu.roll(u, 2, axis=0)[SUBLANES:]
        y = y * _sigmoid(y)
        if part < 2:
            for hd in range(GDN_HEADS):
                sl = slice(hd * GDN_HEAD_DIM, (hd + 1) * GDN_HEAD_DIM)
                seg = y[:, sl]
                outs[part][:, sl] = seg * lax.rsqrt(jnp.sum(seg * seg, axis=-1, keepdims=True) + EPS)
        else:
            v_ref[...] = y
    gate_ref[...] = pending

    ab = _dot(xn, wab_ref[...])
    a_log = ab_ref[0:1, :]
    dt_bias = ab_ref[1:2, :]
    g = -jnp.exp(a_log) * _softplus(ab + dt_bias)
    beta = _sigmoid(ab)
    lane = lax.broadcasted_iota(jnp.int32, ab.shape, 1)
    pos = tb * tm + lax.broadcasted_iota(jnp.int32, ab.shape, 0)
    gb_ref[...] = jnp.where(pos >= FRONT, jnp.where(lane < GDN_HEADS, g, beta), 0.0)


def _gdn_in(x, head, ng, w_main, w_ab, conv_w, ab_par, *, tm, lp):
    batch = x.shape[0]
    m = batch * lp
    tiles = lp // tm
    row = lambda b, t: (b * tiles + t, 0)
    const = lambda b, t: (0, 0)
    wide = pl.BlockSpec((tm, GDN_WIDTH), row)
    return pl.pallas_call(
        functools.partial(_gdn_in_kernel, tm=tm),
        grid=(batch, tiles),
        in_specs=[
            pl.BlockSpec((pl.Element(1), pl.Element(tm), pl.Element(D_MODEL)),
                         lambda b, t: (b, pl.multiple_of(jnp.maximum(t * tm - GCHUNK, 0), GCHUNK), 0)),
            pl.BlockSpec((GCHUNK, D_MODEL), const),
            pl.BlockSpec((1, D_MODEL), const),
            pl.BlockSpec((4 * GDN_WIDTH, D_MODEL), const),
            pl.BlockSpec((D_MODEL, LANES), const),
            pl.BlockSpec((CONV_WIDTH, 3 * GDN_WIDTH), const),
            pl.BlockSpec((SUBLANES, LANES), const),
        ],
        out_specs=[wide, wide, wide, wide, pl.BlockSpec((tm, LANES), row)],
        out_shape=[jax.ShapeDtypeStruct((m, GDN_WIDTH), F32)] * 4 + [jax.ShapeDtypeStruct((m, LANES), F32)],
        scratch_shapes=[pltpu.VMEM((3, SUBLANES, GDN_WIDTH), F32)],
        compiler_params=_params(("arbitrary", "arbitrary")),
        name="gdn_in",
    )(x, head, ng, w_main, w_ab, conv_w, ab_par)


def _mm(xs, ys):
    return [_dot(x.astype(BF16), y.astype(BF16)) for x, y in zip(xs, ys)]


def _unit_lower_inverses(ns):
    c = ns[0].shape[0]
    ri = lax.broadcasted_iota(jnp.int32, (c, c), 0)
    ci = lax.broadcasted_iota(jnp.int32, (c, c), 1)
    eye = (ri == ci).astype(F32)
    same = lambda s: (ri // s) == (ci // s)
    size = 2
    t0 = [eye - jnp.where(same(size), n, 0.0) for n in ns]
    while size < c:
        link = jnp.logical_and(same(2 * size), jnp.logical_not(same(size)))
        tc = _mm(t0, [jnp.where(link, n, 0.0) for n in ns])
        t0 = [t - u for t, u in zip(t0, _mm(tc, t0))]
        size *= 2
    nt = [_dot(n.astype(BF16), jnp.concatenate(_split(t), axis=1)) for n, t in zip(ns, t0)]
    res = [(eye - t) - (p[:, :c] + p[:, c:]) for t, p in zip(t0, nt)]
    return [t + r for t, r in zip(t0, _mm(t0, res))]


def _gdn_core_kernel(q_ref, k_ref, v_ref, gb_ref, gate_ref, x_ref, head_ref, og_ref, wo_ref,
                     out_ref, s_ref, o_ref, *, batch):
    chunk = pl.program_id(0)

    @pl.when(chunk == 0)
    def _():
        s_ref[...] = jnp.zeros_like(s_ref)

    c = GCHUNK
    ri = lax.broadcasted_iota(jnp.int32, (c, c), 0)
    ci = lax.broadcasted_iota(jnp.int32, (c, c), 1)
    incl = ri >= ci
    strict = ri > ci
    tril = incl.astype(BF16)
    scale = GDN_HEAD_DIM ** -0.5
    heads = [(b, hd) for b in range(batch) for hd in range(GDN_HEADS)]
    sl = lambda hd: slice(hd * GDN_HEAD_DIM, (hd + 1) * GDN_HEAD_DIM)
    ks = [k_ref[b, :, sl(hd)] for b, hd in heads]
    kbs = [k.astype(BF16) for k in ks]
    bmats = [jnp.broadcast_to(gb_ref[b, :, GDN_HEADS + hd:GDN_HEADS + hd + 1], (c, c)) for b, hd in heads]
    gcols = [_dot_exact_lhs(tril, jnp.broadcast_to(gb_ref[b, :, hd:hd + 1], (c, c))) for b, hd in heads]
    qs = [q_ref[b, :, sl(hd)] * scale for b, hd in heads]
    kqks = [_dot_nt(jnp.concatenate([kb, q.astype(BF16)], axis=0), kb) for kb, q in zip(kbs, qs)]
    decays = [jnp.where(incl, jnp.exp(jnp.where(incl, g - g.T, 0.0)), 0.0) for g in gcols]
    ns = [jnp.where(strict, bm * d * kqk[:c], 0.0) for bm, d, kqk in zip(bmats, decays, kqks)]
    ts = _unit_lower_inverses(ns)
    egs = [jnp.exp(g) for g in gcols]
    rhs = [jnp.concatenate([bm * v_ref[b, :, sl(hd)], (bm * eg) * k], axis=1).astype(BF16)
           for bm, eg, k, (b, hd) in zip(bmats, egs, ks, heads)]
    sols = [_dot(t.astype(BF16), r) for t, r in zip(ts, rhs)]
    ats = [kqk[c:] * d for kqk, d in zip(kqks, decays)]
    glasts = [g[c - 1:c, :] for g in gcols]
    wcs = [s[:, GDN_HEAD_DIM:].astype(BF16) for s in sols]
    qds = [(q * eg).astype(BF16) for q, eg in zip(qs, egs)]
    kdts = [(k * jnp.exp(gl - g)).T.astype(BF16) for k, gl, g in zip(ks, glasts, gcols)]
    atbs = [a.astype(BF16) for a in ats]

    ss = [s_ref[i] for i in range(len(heads))]
    sbs = [s.astype(BF16) for s in ss]
    ps = [_dot(jnp.concatenate([wc, qd], axis=0), sb) for wc, qd, sb in zip(wcs, qds, sbs)]
    ubs = [(sol[:, :GDN_HEAD_DIM] - p[:c]).astype(BF16) for sol, p in zip(sols, ps)]
    rs = [_dot(jnp.concatenate([at, kdt], axis=0), ub) for at, kdt, ub in zip(atbs, kdts, ubs)]
    for i in range(len(heads)):
        s_ref[i] = ss[i] * jnp.exp(glasts[i]) + rs[i][c:]
    for i, (b, hd) in enumerate(heads):
        o = ps[i][c:] + rs[i][:c]
        o = o * lax.rsqrt(jnp.mean(o * o, axis=-1, keepdims=True) + EPS) * og_ref[...]
        gate = gate_ref[b, :, sl(hd)]
        o_ref[b * c:(b + 1) * c, sl(hd)] = (o * (gate * _sigmoid(gate))).astype(BF16)
    resid = jnp.where(chunk == 0, jnp.concatenate([head_ref[...]] * batch, axis=0),
                      x_ref[...].reshape(batch * c, D_MODEL))
    out_ref[...] = (resid + _dot(o_ref[...], wo_ref[...])).reshape(batch, c, D_MODEL)


def _gdn_core(q, k, v, gb, gate, x, head, og, wo):
    m = q.shape[0]
    batch = x.shape[0]
    nch = m // GCHUNK // batch
    per_batch = lambda a: a.reshape(batch, m // batch, a.shape[-1])
    row = lambda c: (0, c, 0)
    const = lambda c: (0, 0)
    wide = pl.BlockSpec((batch, GCHUNK, GDN_WIDTH), row)
    out = pl.pallas_call(
        functools.partial(_gdn_core_kernel, batch=batch),
        grid=(nch,),
        in_specs=[wide, wide, wide,
                  pl.BlockSpec((batch, GCHUNK, LANES), row),
                  wide,
                  pl.BlockSpec((batch, GCHUNK, D_MODEL), lambda c: (0, jnp.maximum(c - 1, 0), 0)),
                  pl.BlockSpec((GCHUNK, D_MODEL), const),
                  pl.BlockSpec((1, GDN_HEAD_DIM), const),
                  pl.BlockSpec((GDN_WIDTH, D_MODEL), const)],
        out_specs=pl.BlockSpec((batch, GCHUNK, D_MODEL), row),
        out_shape=jax.ShapeDtypeStruct((batch, m // batch, D_MODEL), F32),
        scratch_shapes=[pltpu.VMEM((batch * GDN_HEADS, GDN_HEAD_DIM, GDN_HEAD_DIM), F32),
                        pltpu.VMEM((batch * GCHUNK, GDN_WIDTH), BF16)],
        compiler_params=_params(("arbitrary",)),
        name="gdn_core",
    )(per_batch(q), per_batch(k), per_batch(v), per_batch(gb), per_batch(gate), x, head, og, wo)
    return out.reshape(m, D_MODEL)


def _ffn_chunks():
    step = 3 * MXU_DIM
    return [(lo, min(lo + step, D_FF)) for lo in range(0, D_FF, step)]


def _ffn_body(x, g_ref, wgu_ref, wd_ref):
    xn = _rms(x, g_ref[...]).astype(BF16)

    def gate_up(lo, hi):
        return _dot(xn, wgu_ref[:, lo:hi]), _dot(xn, wgu_ref[:, D_FF + lo:D_FF + hi])

    chunks = _ffn_chunks()
    acc = x
    pending = gate_up(*chunks[0])
    for i, (lo, hi) in enumerate(chunks):
        gate, up = pending
        if i + 1 < len(chunks):
            pending = gate_up(*chunks[i + 1])
        act = (gate * _sigmoid(gate) * up).astype(BF16)
        acc = acc + _dot(act, wd_ref[lo:hi, :])
    return acc


def _ffn_kernel(x_ref, g_ref, wgu_ref, wd_ref, out_ref):
    out_ref[...] = _ffn_body(x_ref[...], g_ref, wgu_ref, wd_ref)


def _ffn_final_kernel(x_ref, g_ref, wgu_ref, wd_ref, gf_ref, out_ref):
    out_ref[0] = _rms(_ffn_body(x_ref[...], g_ref, wgu_ref, wd_ref), gf_ref[...])


def _ffn_weight_specs(layer):
    pick = lambda *_: (layer, 0, 0)
    return [pl.BlockSpec((1, D_MODEL), lambda *_: (0, 0)),
            pl.BlockSpec((None, D_MODEL, 2 * D_FF), pick, pipeline_mode=pl.Buffered(1)),
            pl.BlockSpec((None, D_FF, D_MODEL), pick, pipeline_mode=pl.Buffered(1))]


def _ffn(x, g, w_gu, w_d, *, layer, tm):
    m = x.shape[0]
    row = lambda r: (r, 0)
    return pl.pallas_call(
        _ffn_kernel,
        grid=(m // tm,),
        in_specs=[pl.BlockSpec((tm, D_MODEL), row)] + _ffn_weight_specs(layer),
        out_specs=pl.BlockSpec((tm, D_MODEL), row),
        out_shape=jax.ShapeDtypeStruct((m, D_MODEL), F32),
        compiler_params=_params(("parallel",)),
        name="ffn",
    )(x, g, w_gu, w_d)


def _ffn_final(x, g, w_gu, w_d, gf, *, layer, batch, lp, seq):
    rows = max(r for r in (1024, 512, 256, 128) if seq % r == 0)
    const = lambda b, r: (0, 0)
    return pl.pallas_call(
        _ffn_final_kernel,
        grid=(batch, seq // rows),
        in_specs=[pl.BlockSpec((pl.Element(rows), pl.Element(D_MODEL)),
                               lambda b, r: (pl.multiple_of(b * lp + GCHUNK + r * rows, GCHUNK), 0))]
        + _ffn_weight_specs(layer) + [pl.BlockSpec((1, D_MODEL), const)],
        out_specs=pl.BlockSpec((1, rows, D_MODEL), lambda b, r: (b, r, 0)),
        out_shape=jax.ShapeDtypeStruct((batch, seq, D_MODEL), F32),
        compiler_params=_params(("parallel", "parallel")),
        name="ffn_final",
    )(x, g, w_gu, w_d, gf)


def _qkv_kernel(x_ref, gkv_ref, gq_ref, wkv_ref, wq_ref, q_ref, k_ref, v_ref):
    x = x_ref[...]
    inv = lax.rsqrt(jnp.mean(x * x, axis=-1, keepdims=True) + EPS)
    xkv = (x * inv * gkv_ref[...]).astype(BF16)
    xq = (x * inv * gq_ref[...]).astype(BF16)
    k_ref[...] = _dot(xkv, wkv_ref[:, :SB_WIDTH]).astype(BF16)
    v_ref[...] = _dot(xkv, wkv_ref[:, SB_WIDTH:]).astype(BF16)
    q_ref[...] = (_dot(xq, wq_ref[...]) * (SB_HEAD_DIM ** -0.5)).astype(BF16)


def _qkv(x, gkv, gq, wkv, wq, *, tm):
    m = x.shape[0]
    row = lambda r: (r, 0)
    const = lambda r: (0, 0)
    out = pl.BlockSpec((tm, SB_WIDTH), row)
    return pl.pallas_call(
        _qkv_kernel,
        grid=(m // tm,),
        in_specs=[pl.BlockSpec((tm, D_MODEL), row),
                  pl.BlockSpec((1, D_MODEL), const),
                  pl.BlockSpec((1, D_MODEL), const),
                  pl.BlockSpec((D_MODEL, 2 * SB_WIDTH), const),
                  pl.BlockSpec((D_MODEL, SB_WIDTH), const)],
        out_specs=[out, out, out],
        out_shape=[jax.ShapeDtypeStruct((m, SB_WIDTH), BF16)] * 3,
        compiler_params=_params(("parallel",)),
        name="sb_qkv",
    )(x, gkv, gq, wkv, wq)


def _sb_kernel(q_ref, k_ref, v_ref, h_ref, wo_ref, out_ref, c_ref, o_ref, obf_ref, *, group):
    tq = SB_BLOCK
    iq0 = pl.program_id(1) * group
    first = lax.broadcasted_iota(jnp.int32, (tq, LANES), 1) < SB_HEAD_DIM
    ri = lax.broadcasted_iota(jnp.int32, (tq, tq), 0)
    ci = lax.broadcasted_iota(jnp.int32, (tq, tq), 1)
    upper_ext = jnp.concatenate([(ri > ci).astype(BF16), jnp.ones((tq, LANES), BF16)], axis=1)
    pairs = range(SB_PAIRS)
    cols = [slice(p * LANES, (p + 1) * LANES) for p in pairs]

    c_ref[...] = jnp.zeros_like(c_ref)
    o_ref[...] = jnp.zeros_like(o_ref)

    def visit(s, j, bias, lo=0, hi=tq):
        start = pl.multiple_of(j * tq, tq)
        slot = [s * SB_PAIRS + p for p in pairs]
        rows = hi - lo
        first_q = lax.broadcasted_iota(jnp.int32, (rows, LANES), 1) < SB_HEAD_DIM

        def stacked_q(p):
            q2 = q_ref[s * tq + lo:s * tq + hi, cols[p]]
            zero = jnp.zeros_like(q2)
            return jnp.concatenate([jnp.where(first_q, q2, zero), jnp.where(first_q, zero, q2)], axis=0)

        zs = [_dot_nt(stacked_q(p), k_ref[pl.ds(start, tq), cols[p]]) + bias for p in pairs]
        sps = [_softplus(z) for z in zs]
        accs = [_dot(sp.astype(BF16), upper_ext) for sp in sps]
        ws = []
        cmin = None
        for p in pairs:
            c = jnp.concatenate([c_ref[slot[p], lo:hi, :], c_ref[slot[p], tq + lo:tq + hi, :]], axis=0)
            ws.append(jnp.exp(zs[p] - sps[p] - accs[p][:, :tq] - c).astype(BF16))
            c = c + accs[p][:, tq:]
            c_ref[slot[p], lo:hi, :] = c[:rows]
            c_ref[slot[p], tq + lo:tq + hi, :] = c[rows:]
            cmin = c if cmin is None else jnp.minimum(cmin, c)
        for p in pairs:
            vj = v_ref[pl.ds(start, tq), cols[p]]
            zero = jnp.zeros_like(vj)
            v2 = jnp.concatenate([jnp.where(first, vj, zero), jnp.where(first, zero, vj)], axis=0)
            o_ref[slot[p], lo:hi, :] += _dot(jnp.concatenate([ws[p][:rows], ws[p][rows:]], axis=1), v2)
        return cmin

    def row_bias(j, live):
        kp = j * tq + lax.broadcasted_iota(jnp.int32, (1, tq), 1)
        return jnp.where(jnp.logical_and(kp >= FRONT, live), 0.0, SB_MASKED)

    for s in range(group):
        kpos = (iq0 + s) * tq + ci
        diag = jnp.where(jnp.logical_and(ci < ri, kpos >= FRONT), 0.0, SB_MASKED)
        visit(s, iq0 + s, jnp.concatenate([diag, diag], axis=0))

    rests = []
    for s in range(group):
        prev = jnp.maximum(iq0 + s - 1, 0)
        cmin = visit(s, prev, row_bias(prev, iq0 + s >= 1))
        rests.append(jnp.min(jnp.minimum(cmin[SB_TOP:tq], cmin[tq + SB_TOP:])))

    tops = []
    for s in range(group):
        prev = jnp.maximum(iq0 + s - 2, 0)
        tops.append(jnp.min(visit(s, prev, row_bias(prev, iq0 + s >= 2), hi=SB_TOP)))

    def cond(carry):
        j, low = carry
        return jnp.logical_and(j >= 0, low < SB_SKIP)

    for s in range(group):
        def rest_body(carry, s=s):
            j = carry[0]
            return j - 1, jnp.min(visit(s, j, row_bias(j, True), lo=SB_TOP))

        def top_body(carry, s=s):
            j = carry[0]
            return j - 1, jnp.min(visit(s, j, row_bias(j, True), hi=SB_TOP))

        lax.while_loop(cond, rest_body, (iq0 + s - 2, rests[s]))
        lax.while_loop(cond, top_body, (iq0 + s - 3, tops[s]))

    for s in range(group):
        for p in pairs:
            obf_ref[s * tq:(s + 1) * tq, cols[p]] = o_ref[s * SB_PAIRS + p].astype(BF16)
    out_ref[...] = h_ref[...] + _dot(obf_ref[...], wo_ref[...])


def _sb_layer(h, q, k, v, wo, *, batch, lp):
    m = q.shape[0]
    nq = lp // SB_BLOCK
    tq = SB_BLOCK
    group = max(g for g in (5, 4, 3, 2, 1) if nq % g == 0)
    steps = nq // group
    row = lambda b, i: (b * steps + i, 0)
    whole = lambda b, i: (b, 0)
    return pl.pallas_call(
        functools.partial(_sb_kernel, group=group),
        grid=(batch, steps),
        in_specs=[pl.BlockSpec((group * tq, SB_WIDTH), row),
                  pl.BlockSpec((lp, SB_WIDTH), whole, pipeline_mode=pl.Buffered(1)),
                  pl.BlockSpec((lp, SB_WIDTH), whole, pipeline_mode=pl.Buffered(1)),
                  pl.BlockSpec((group * tq, D_MODEL), row),
                  pl.BlockSpec((SB_WIDTH, D_MODEL), lambda b, i: (0, 0), pipeline_mode=pl.Buffered(1))],
        out_specs=pl.BlockSpec((group * tq, D_MODEL), row, pipeline_mode=pl.Buffered(1)),
        out_shape=jax.ShapeDtypeStruct((m, D_MODEL), F32),
        scratch_shapes=[pltpu.VMEM((group * SB_PAIRS, 2 * tq, tq), F32),
                        pltpu.VMEM((group * SB_PAIRS, tq, LANES), F32),
                        pltpu.VMEM((group * tq, SB_WIDTH), BF16)],
        compiler_params=_params(("parallel", "arbitrary")),
        name="sb_attn",
    )(q, k, v, h, wo)


def _row_tile(lp):
    for tm in (640, 512, 384, 256, 128):
        if lp % tm == 0:
            return tm
    raise ValueError(f"padded length {lp} is not a multiple of {GCHUNK}")


def kernel(x, meta_tokens, gdn_norm_g, gdn_w_in, gdn_conv_w, gdn_a_log, gdn_dt_bias, gdn_onorm_g, gdn_w_out,
           kv_norm_g, w_kv, sb_norm_g, sb_w_q, sb_w_o, ffn_norm_g, ffn_w_gate_up, ffn_w_down, final_norm_g):
    batch, seq, _ = x.shape
    assert seq % GCHUNK == 0 and gdn_w_in.shape[0] == 1 and sb_w_q.shape[0] == 1
    lp = GCHUNK + seq
    tm = _row_tile(lp)

    assert tm > GCHUNK and seq >= tm
    head = jnp.concatenate([jnp.zeros((FRONT, D_MODEL), x.dtype), meta_tokens.astype(x.dtype)], axis=0)

    w_main = _to_bf16(jnp.swapaxes(gdn_w_in[0], 0, 1))
    w_ab = jnp.pad(gdn_w_in[0][:, 4 * GDN_WIDTH:], ((0, 0), (0, LANES - 2 * GDN_HEADS))).astype(BF16)
    ab_par = jnp.zeros((SUBLANES, LANES), F32)
    ab_par = ab_par.at[0, :GDN_HEADS].set(gdn_a_log[0]).at[1, :GDN_HEADS].set(gdn_dt_bias[0])
    w_gu = _to_bf16(ffn_w_gate_up)
    w_down = _to_bf16(ffn_w_down)

    q, k, v, gate, gb = _gdn_in(x, head, gdn_norm_g[0][None], w_main, w_ab, gdn_conv_w[0], ab_par, tm=tm, lp=lp)
    h = _gdn_core(q, k, v, gb, gate, x, head, gdn_onorm_g[0][None], _to_bf16(gdn_w_out)[0])
    h = _ffn(h, ffn_norm_g[0][None], w_gu, w_down, layer=0, tm=tm)

    sq, sk, sv = _qkv(h, kv_norm_g[None], sb_norm_g[0][None], _to_bf16(w_kv), _to_bf16(sb_w_q)[0], tm=tm)
    h = _sb_layer(h, sq, sk, sv, _to_bf16(sb_w_o)[0], batch=batch, lp=lp)
    return _ffn_final(h, ffn_norm_g[1][None], w_gu, w_down, final_norm_g[None], layer=1, batch=batch, lp=lp, seq=seq)
```
